```python
import math
import jax, jax.numpy as jnp
from jax import lax
import numpy as np

D_MODEL = 4096
BATCH = 4
SEQ = 2048
DEPTH = 1
DEC_BATCH = 32
DEC_SEQ = 8
PAST_LEN = 8192
PAGE_SIZE = 128

HEAD_DIM = 128
N_HEADS_A = D_MODEL // (4 * HEAD_DIM)
N_HEADS_B = D_MODEL // (2 * HEAD_DIM)
WIDTH_A = N_HEADS_A * 2 * HEAD_DIM
WIDTH_B = N_HEADS_B * HEAD_DIM
D_IN = 3 * WIDTH_A + 3 * WIDTH_B
N_META = 16
BLOCK = 128
ROPE_THETA = 10000.0
N_GROUPS = 4
EXPERTS_PER_GROUP = 8
N_EXPERTS = N_GROUPS * EXPERTS_PER_GROUP
TOP_K = 2
D_FF_EXPERT = D_MODEL // 4
NORM_EPS = 1e-6
SUBLN_EPS = 1e-5
NEG_INF = -1e30

kernel_name = "hymba_diff_stickbreak_hmoe_step"


def rmsnorm(x, g, eps=NORM_EPS):
    xf = x.astype(jnp.float32)
    y = xf * lax.rsqrt(jnp.mean(xf * xf, axis=-1, keepdims=True) + eps)
    return (y * g.astype(jnp.float32)).astype(x.dtype)


def rope(x, pos):
    half = HEAD_DIM // 2
    inv = ROPE_THETA ** (-jnp.arange(half, dtype=jnp.float32) / half)
    ang = pos.astype(jnp.float32)[:, None] * inv[None, :]
    shape = (1, ang.shape[0]) + (1,) * (x.ndim - 3) + (half,)
    cos = jnp.cos(ang).reshape(shape).astype(x.dtype)
    sin = jnp.sin(ang).reshape(shape).astype(x.dtype)
    x1, x2 = x[..., :half], x[..., half:]
    return jnp.concatenate([x1 * cos - x2 * sin, x2 * cos + x1 * sin], axis=-1)


def project(n, w_in, pos):
    b, t, _ = n.shape
    p = n @ w_in
    cuts = [WIDTH_A, 2 * WIDTH_A, 3 * WIDTH_A, 3 * WIDTH_A + WIDTH_B, 3 * WIDTH_A + 2 * WIDTH_B]
    qa, ka, va, qb, kb, vb = jnp.split(p, cuts, axis=-1)
    qa = rope(qa.reshape(b, t, N_HEADS_A, 2, HEAD_DIM), pos)
    ka = rope(ka.reshape(b, t, N_HEADS_A, 2, HEAD_DIM), pos)
    va = va.reshape(b, t, N_HEADS_A, 2 * HEAD_DIM)
    qb = qb.reshape(b, t, N_HEADS_B, HEAD_DIM)
    kb = kb.reshape(b, t, N_HEADS_B, HEAD_DIM)
    vb = vb.reshape(b, t, N_HEADS_B, HEAD_DIM)
    return qa, ka, va, qb, kb, vb


def diff_core(qa, ka, va, q_idx, k_idx, k_valid, lam, lam_init, subln_g):
    s = jnp.einsum('bqhcd,bkhcd->bhcqk', qa, ka, preferred_element_type=jnp.float32) * (HEAD_DIM ** -0.5)
    mask = (k_idx[None, :] <= q_idx[:, None]) & k_valid[None, :]
    p = jax.nn.softmax(jnp.where(mask, s, NEG_INF), axis=-1)
    w = p[:, :, 0] - lam * p[:, :, 1]
    o = jnp.einsum('bhqk,bkhe->bqhe', w.astype(va.dtype), va)
    return rmsnorm(o, subln_g, SUBLN_EPS) * (1.0 - lam_init)


def sb_core(qb, kb, vb, q_idx, k_idx, k_valid):
    z = jnp.einsum('bqhd,bkhd->bhqk', qb, kb, preferred_element_type=jnp.float32) * (HEAD_DIM ** -0.5)
    mask = (k_idx[None, :] < q_idx[:, None]) & k_valid[None, :]
    log_1m = jnp.where(mask, jax.nn.log_sigmoid(-z), 0.0)
    later = lax.cumsum(log_1m, axis=3, reverse=True) - log_1m
    a = jnp.where(mask, jnp.exp(jax.nn.log_sigmoid(z) + later), 0.0)
    return jnp.einsum('bhqk,bkhd->bqhd', a.astype(vb.dtype), vb)


def prompt_mixers(qa, ka, va, qb, kb, vb, lam, lam_init, subln_g):
    b, t = qa.shape[:2]
    pad = (-N_META) % BLOCK
    tp = t + pad
    nb = tp // BLOCK
    padf = lambda a: jnp.pad(a, [(0, 0), (pad, 0)] + [(0, 0)] * (a.ndim - 2))
    qa, ka, va, qb, kb, vb = padf(qa), padf(ka), padf(va), padf(qb), padf(kb), padf(vb)
    idx = jnp.arange(tp)
    valid = idx >= pad
    to_blocks = lambda a: jnp.moveaxis(a.reshape((b, nb, BLOCK) + a.shape[2:]), 1, 0)

    def one_block(args):
        qa_blk, qb_blk, q_idx = args
        oa = diff_core(qa_blk, ka, va, q_idx, idx, valid, lam, lam_init, subln_g)
        ob = sb_core(qb_blk, kb, vb, q_idx, idx, valid)
        return oa, ob

    oa, ob = lax.map(one_block, (to_blocks(qa), to_blocks(qb), idx.reshape(nb, BLOCK)))
    from_blocks = lambda a: jnp.moveaxis(a, 0, 1).reshape((b, tp) + a.shape[3:])[:, pad:]
    return from_blocks(oa), from_blocks(ob)


def sample_mixers(layer, qa, ka, va, qb, kb, vb, cache_a_k, cache_a_v, cache_b_k, cache_b_v,
                  page_table, lam, lam_init, subln_g):
    ds = qa.shape[1]
    k_idx = jnp.arange(PAST_LEN + ds)
    valid = jnp.ones((PAST_LEN + ds,), dtype=bool)
    q_idx = PAST_LEN + jnp.arange(ds)

    def past(cache, pages):
        rows = cache[layer, pages]
        return rows.reshape((PAST_LEN,) + rows.shape[2:])

    def one_seq(args):
        pages, qa1, ka1, va1, qb1, kb1, vb1 = args
        ka_all = jnp.concatenate([past(cache_a_k, pages), ka1], axis=0)[None]
        va_all = jnp.concatenate([past(cache_a_v, pages), va1], axis=0)[None]
        kb_all = jnp.concatenate([past(cache_b_k, pages), kb1], axis=0)[None]
        vb_all = jnp.concatenate([past(cache_b_v, pages), vb1], axis=0)[None]
        oa = diff_core(qa1[None], ka_all, va_all, q_idx, k_idx, valid, lam, lam_init, subln_g)[0]
        ob = sb_core(qb1[None], kb_all, vb_all, q_idx, k_idx, valid)[0]
        return oa, ob

    return lax.map(one_seq, (page_table, qa, ka, va, qb, kb, vb))


def hier_moe(x, w_rg, w_re, w_gate, w_up, w_down, layer):
    shp = x.shape
    h = x.reshape(-1, D_MODEL)
    pg = jax.nn.softmax(jnp.einsum('nd,dg->ng', h, w_rg, preferred_element_type=jnp.float32), axis=-1)
    g_idx = jnp.argmax(pg, axis=-1)
    g_gate = jnp.take_along_axis(pg, g_idx[:, None], axis=-1)
    le = jnp.einsum('nd,dge->nge', h, w_re, preferred_element_type=jnp.float32)
    le = jnp.take_along_axis(le, g_idx[:, None, None], axis=1)[:, 0]
    top_p, top_e = lax.top_k(jax.nn.softmax(le, axis=-1), TOP_K)
    wts = g_gate * top_p / jnp.sum(top_p, axis=-1, keepdims=True)
    expert_id = g_idx[:, None] * EXPERTS_PER_GROUP + top_e
    combine = jnp.sum(jax.nn.one_hot(expert_id, N_EXPERTS, dtype=jnp.float32) * wts[..., None], axis=1)
    out = jnp.zeros_like(h)
    for e in range(N_EXPERTS):
        a = h @ w_gate[layer, e]
        u = h @ w_up[layer, e]
        out = out + combine[:, e:e + 1].astype(h.dtype) * ((jax.nn.silu(a) * u) @ w_down[layer, e])
    return out.reshape(shp)


def setup_inputs(seed: int = 0) -> dict:
    key = jax.random.key(seed)
    ks = jax.random.split(key, 24)
    f = jnp.float32
    n_pages = PAST_LEN // PAGE_SIZE
    n_used = DEC_BATCH * n_pages
    n_pool = n_used + (n_used + 3) // 4
    nrm = lambda k, shape, scale=1.0: jax.random.normal(k, shape, f) * scale
    page_table = jax.random.permutation(ks[0], n_pool)[:n_used].reshape(DEC_BATCH, n_pages).astype(jnp.int32)
    return {
        'x_prompt': nrm(ks[1], (BATCH, SEQ, D_MODEL)),
        'x_sample': nrm(ks[2], (DEC_BATCH, DEC_SEQ, D_MODEL)),
        'cache_a_k': nrm(ks[3], (DEPTH, n_pool, PAGE_SIZE, N_HEADS_A, 2, HEAD_DIM)),
        'cache_a_v': nrm(ks[4], (DEPTH, n_pool, PAGE_SIZE, N_HEADS_A, 2 * HEAD_DIM)),
        'cache_b_k': nrm(ks[5], (DEPTH, n_pool, PAGE_SIZE, N_HEADS_B, HEAD_DIM)),
        'cache_b_v': nrm(ks[6], (DEPTH, n_pool, PAGE_SIZE, N_HEADS_B, HEAD_DIM)),
        'page_table': page_table,
        'meta_tokens': nrm(ks[7], (N_META, D_MODEL)),
        'attn_norm': 1.0 + nrm(ks[8], (DEPTH, D_MODEL), 0.02),
        'w_in': nrm(ks[9], (DEPTH, D_MODEL, D_IN), D_MODEL ** -0.5),
        'lam_q1': nrm(ks[10], (DEPTH, HEAD_DIM), 0.1),
        'lam_k1': nrm(ks[11], (DEPTH, HEAD_DIM), 0.1),
        'lam_q2': nrm(ks[12], (DEPTH, HEAD_DIM), 0.1),
        'lam_k2': nrm(ks[13], (DEPTH, HEAD_DIM), 0.1),
        'subln_norm': 1.0 + nrm(ks[14], (DEPTH, 2 * HEAD_DIM), 0.02),
        'w_out': nrm(ks[15], (DEPTH, WIDTH_A + WIDTH_B, D_MODEL), (WIDTH_A + WIDTH_B) ** -0.5),
        'ffn_norm': 1.0 + nrm(ks[16], (DEPTH, D_MODEL), 0.02),
        'w_router_group': nrm(ks[17], (DEPTH, D_MODEL, N_GROUPS), D_MODEL ** -0.5),
        'w_router_expert': nrm(ks[18], (DEPTH, D_MODEL, N_GROUPS, EXPERTS_PER_GROUP), D_MODEL ** -0.5),
        'w_gate': nrm(ks[19], (DEPTH, N_EXPERTS, D_MODEL, D_FF_EXPERT), D_MODEL ** -0.5),
        'w_up': nrm(ks[20], (DEPTH, N_EXPERTS, D_MODEL, D_FF_EXPERT), D_MODEL ** -0.5),
        'w_down': nrm(ks[21], (DEPTH, N_EXPERTS, D_FF_EXPERT, D_MODEL), D_FF_EXPERT ** -0.5),
        'final_norm': 1.0 + nrm(ks[22], (D_MODEL,), 0.02),
    }


def reference(x_prompt, x_sample, cache_a_k, cache_a_v, cache_b_k, cache_b_v, page_table,
              meta_tokens, attn_norm, w_in, lam_q1, lam_k1, lam_q2, lam_k2, subln_norm, w_out,
              ffn_norm, w_router_group, w_router_expert, w_gate, w_up, w_down, final_norm):
    f = jnp.float32
    b = x_prompt.shape[0]
    meta = jnp.broadcast_to(meta_tokens[None].astype(x_prompt.dtype), (b, N_META, D_MODEL))
    hp = jnp.concatenate([meta, x_prompt], axis=1)
    hs = x_sample
    pos_p = jnp.arange(hp.shape[1])
    pos_s = PAST_LEN + jnp.arange(hs.shape[1])
    akp, avp, bkp, bvp, aks, avs, bks, bvs = [], [], [], [], [], [], [], []
    for layer in range(DEPTH):
        lam_init = 0.8 - 0.6 * math.exp(-0.3 * layer)
        lam = (jnp.exp(jnp.sum(lam_q1[layer].astype(f) * lam_k1[layer].astype(f)))
               - jnp.exp(jnp.sum(lam_q2[layer].astype(f) * lam_k2[layer].astype(f))) + lam_init)
        qa, ka, va, qb, kb, vb = project(rmsnorm(hp, attn_norm[layer]), w_in[layer], pos_p)
        oa, ob = prompt_mixers(qa, ka, va, qb, kb, vb, lam, lam_init, subln_norm[layer])
        t = hp.shape[1]
        mix = jnp.concatenate([oa.reshape(b, t, WIDTH_A), ob.reshape(b, t, WIDTH_B)], axis=-1)
        hp = hp + mix @ w_out[layer]
        hp = hp + hier_moe(rmsnorm(hp, ffn_norm[layer]), w_router_group[layer], w_router_expert[layer],
                           w_gate, w_up, w_down, layer)
        akp.append(ka); avp.append(va); bkp.append(kb); bvp.append(vb)
        qa, ka, va, qb, kb, vb = project(rmsnorm(hs, attn_norm[layer]), w_in[layer], pos_s)
        oa, ob = sample_mixers(layer, qa, ka, va, qb, kb, vb, cache_a_k, cache_a_v, cache_b_k, cache_b_v,
                               page_table, lam, lam_init, subln_norm[layer])
        db, ds = hs.shape[:2]
        mix = jnp.concatenate([oa.reshape(db, ds, WIDTH_A), ob.reshape(db, ds, WIDTH_B)], axis=-1)
        hs = hs + mix @ w_out[layer]
        hs = hs + hier_moe(rmsnorm(hs, ffn_norm[layer]), w_router_group[layer], w_router_expert[layer],
                           w_gate, w_up, w_down, layer)
        aks.append(ka); avs.append(va); bks.append(kb); bvs.append(vb)
    y_prompt = rmsnorm(hp, final_norm)[:, N_META:]
    y_sample = rmsnorm(hs, final_norm)
    return (y_prompt, y_sample,
            jnp.stack(akp), jnp.stack(avp), jnp.stack(bkp), jnp.stack(bvp),
            jnp.stack(aks), jnp.stack(avs), jnp.stack(bks), jnp.stack(bvs))
```

```python
import functools
import math

import jax
import jax.numpy as jnp
from jax import lax
from jax.experimental import pallas as pl
from jax.experimental.pallas import tpu as pltpu

F32 = jnp.float32
BF16 = jnp.bfloat16

HEAD_DIM = 128
N_META = 16
ROPE_THETA = 10000.0
NORM_EPS = 1e-6
SUBLN_EPS = 1e-5
NEG_INF = -1e30
N_GROUPS = 4
EXPERTS_PER_GROUP = 8
N_EXPERTS = N_GROUPS * EXPERTS_PER_GROUP
QK_SCALE = HEAD_DIM ** -0.5

VMEM_LIMIT_BYTES = 56 * 1024 * 1024


def _cparams(sem):
    return pltpu.CompilerParams(dimension_semantics=sem, vmem_limit_bytes=VMEM_LIMIT_BYTES)


def _inproj_body(x_ref, g_ref, w_ref, cos_ref, sin_ref, pb_ref, kv_ref, n_scr, *, tn, width_a, width_b):
    j = pl.program_id(1)

    @pl.when(j == 0)
    def _():
        x = x_ref[...]
        ms = jnp.mean(x * x, axis=-1, keepdims=True)
        n_scr[...] = (x * lax.rsqrt(ms + NORM_EPS) * g_ref[...]).astype(BF16)

    acc = jnp.dot(n_scr[...], w_ref[...], preferred_element_type=F32)
    col = j * tn
    is_rope = col < 2 * width_a
    is_q = (col < width_a) | ((col >= 3 * width_a) & (col < 3 * width_a + width_b))
    is_kv = jnp.logical_not(is_q)

    def emit(val):
        sc = jnp.where(is_q, QK_SCALE, 1.0).astype(F32)
        pb_ref[...] = (val * sc).astype(BF16)

        @pl.when(is_kv)
        def _():
            kv_ref[...] = val

    @pl.when(is_rope)
    def _():
        cos = cos_ref[...]
        sin = sin_ref[...]
        chunks = []
        for c in range(tn // HEAD_DIM):
            a = acc[:, c * HEAD_DIM:(c + 1) * HEAD_DIM]
            chunks.append(a * cos + pltpu.roll(a, HEAD_DIM // 2, 1) * sin)
        emit(jnp.concatenate(chunks, axis=1))

    @pl.when(jnp.logical_not(is_rope))
    def _():
        emit(acc)


def _inproj(x, g, w_bf, cos2, sin2, *, tm, tn, width_a, width_b):
    r, d = x.shape
    d_in = w_bf.shape[1]
    assert r % tm == 0 and d_in % tn == 0 and width_a % tn == 0 and width_b % tn == 0
    na, nb = width_a // tn, width_b // tn

    def kv_map(i, j):
        ja = jnp.clip(j - na, 0, 2 * na - 1)
        jb = jnp.clip(j - (3 * na + nb), 0, 2 * nb - 1) + 2 * na
        return (i, jnp.where(j < 3 * na, ja, jb))

    body = functools.partial(_inproj_body, tn=tn, width_a=width_a, width_b=width_b)
    return pl.pallas_call(
        body,
        grid=(r // tm, d_in // tn),
        in_specs=[
            pl.BlockSpec((tm, d), lambda i, j: (i, 0)),
            pl.BlockSpec((1, d), lambda i, j: (0, 0)),
            pl.BlockSpec((d, tn), lambda i, j: (0, j)),
            pl.BlockSpec((tm, HEAD_DIM), lambda i, j: (i, 0)),
            pl.BlockSpec((tm, HEAD_DIM), lambda i, j: (i, 0)),
        ],
        out_specs=[
            pl.BlockSpec((tm, tn), lambda i, j: (i, j)),
            pl.BlockSpec((tm, tn), kv_map),
        ],
        out_shape=[
            jax.ShapeDtypeStruct((r, d_in), BF16),
            jax.ShapeDtypeStruct((r, 2 * width_a + 2 * width_b), F32),
        ],
        scratch_shapes=[pltpu.VMEM((tm, d), BF16)],
        compiler_params=_cparams(("parallel", "arbitrary")),
        name="inproj",
    )(x, g, w_bf, cos2, sin2)


def _rope_tables(pos):
    half = HEAD_DIM // 2
    inv = ROPE_THETA ** (-jnp.arange(half, dtype=F32) / half)
    ang = pos.astype(F32)[:, None] * inv[None, :]
    cos, sin = jnp.cos(ang), jnp.sin(ang)
    return jnp.concatenate([cos, cos], axis=1), jnp.concatenate([-sin, sin], axis=1)


def _subln(d, g, lam_init):
    ms = jnp.mean(d * d, axis=-1, keepdims=True)
    return d * lax.rsqrt(ms + SUBLN_EPS) * g * (1.0 - lam_init)


def _attn_a_body(lam_ref, q_ref, k_ref, v_ref, mk_ref, mv_ref, g_ref, o_ref, m_scr, l_scr, acc_scr,
                 *, tq, lam_init):
    qi = pl.program_id(2)
    m_scr[...] = jnp.full(m_scr.shape, NEG_INF, F32)
    l_scr[...] = jnp.zeros(l_scr.shape, F32)
    acc_scr[...] = jnp.zeros(acc_scr.shape, F32)

    def step(k_blk, v_blk, mask):
        for c in range(2):
            q = q_ref[:, c * HEAD_DIM:(c + 1) * HEAD_DIM]
            kc = k_blk[:, c * HEAD_DIM:(c + 1) * HEAD_DIM]
            s = lax.dot_general(q, kc, (((1,), (1,)), ((), ())), preferred_element_type=F32)
            if mask is not None:
                s = jnp.where(mask, s, NEG_INF)
            m_prev = m_scr[c]
            m_new = jnp.maximum(m_prev, jnp.max(s, axis=1, keepdims=True))
            alpha = jnp.exp(m_prev - m_new)
            p = jnp.exp(s - m_new)
            l_scr[c] = alpha * l_scr[c] + jnp.sum(p, axis=1, keepdims=True)
            acc_scr[c] = alpha * acc_scr[c] + jnp.dot(p.astype(BF16), v_blk, preferred_element_type=F32)
            m_scr[c] = m_new

    step(mk_ref[...], mv_ref[...], None)

    def body(kb, carry):
        off = pl.multiple_of(kb * tq, tq)
        step(k_ref[pl.ds(off, tq), :], v_ref[pl.ds(off, tq), :], None)
        return carry

    lax.fori_loop(0, qi, body, 0)
    off = pl.multiple_of(qi * tq, tq)
    row = lax.broadcasted_iota(jnp.int32, (tq, tq), 0)
    colk = lax.broadcasted_iota(jnp.int32, (tq, tq), 1)
    step(k_ref[pl.ds(off, tq), :], v_ref[pl.ds(off, tq), :], colk <= row)

    lam = lam_ref[0, 0]
    o0 = acc_scr[0] / l_scr[0]
    o1 = acc_scr[1] / l_scr[1]
    o_ref[...] = _subln(o0 - lam * o1, g_ref[...], lam_init).astype(o_ref.dtype)


def _attn_a(lam, pb, pb_meta, g, *, batch, seq, n_heads, col_q, col_k, col_v, tq, lam_init):
    w = 2 * HEAD_DIM
    nq = seq // tq
    body = functools.partial(_attn_a_body, tq=tq, lam_init=lam_init)
    return pl.pallas_call(
        body,
        grid=(batch, n_heads, nq),
        in_specs=[
            pl.BlockSpec(memory_space=pltpu.SMEM),
            pl.BlockSpec((tq, w), lambda b, h, i: (b * nq + i, col_q + h)),
            pl.BlockSpec((seq, w), lambda b, h, i: (b, col_k + h)),
            pl.BlockSpec((seq, w), lambda b, h, i: (b, col_v + h)),
            pl.BlockSpec((N_META, w), lambda b, h, i: (0, col_k + h)),
            pl.BlockSpec((N_META, w), lambda b, h, i: (0, col_v + h)),
            pl.BlockSpec((1, w), lambda b, h, i: (0, 0)),
        ],
        out_specs=pl.BlockSpec((tq, w), lambda b, h, i: (b * nq + i, h)),
        out_shape=jax.ShapeDtypeStruct((batch * seq, n_heads * w), BF16),
        scratch_shapes=[
            pltpu.VMEM((2, tq, 1), F32),
            pltpu.VMEM((2, tq, 1), F32),
            pltpu.VMEM((2, tq, w), F32),
        ],
        compiler_params=_cparams(("parallel", "parallel", "arbitrary")),
        name="attn_a",
    )(lam, pb, pb, pb, pb_meta, pb_meta, g)


def _log_sigmoid_pair(z):
    sp = jnp.log1p(jnp.exp(-jnp.abs(z)))
    return jnp.minimum(z, 0.0) - sp, -jnp.maximum(z, 0.0) - sp


def _split_bf16(x):
    hi = x.astype(BF16)
    lo = (x - hi.astype(F32)).astype(BF16)
    return hi, lo


def _attn_b_body(q_ref, k_ref, v_ref, mk_ref, mv_ref, tri_ref, trim_ref, o_ref, *, tq, tk):
    qi = pl.program_id(2)
    n_sub = tk // tq

    def block(q, k_blk, v_blk, tri, mask, carry, acc):
        z = lax.dot_general(q, k_blk, (((1,), (1,)), ((), ())), preferred_element_type=F32)
        ls, lm = _log_sigmoid_pair(z)
        if mask is not None:
            lm = jnp.where(mask, lm, 0.0)
        hi, lo = _split_bf16(lm)
        both = jnp.dot(jnp.concatenate([hi, lo], axis=0), tri, preferred_element_type=F32)
        later = both[:tq] + both[tq:]
        a = jnp.exp(ls + later + carry)
        if mask is not None:
            a = jnp.where(mask, a, 0.0)
        acc = acc + jnp.dot(a.astype(BF16), v_blk, preferred_element_type=F32)
        carry = carry + jnp.sum(lm, axis=1, keepdims=True)
        return carry, acc

    kd = qi // n_sub
    sub = qi % n_sub
    outs = []
    for hh in range(2):
        lo_c, hi_c = hh * HEAD_DIM, (hh + 1) * HEAD_DIM
        q = q_ref[:, lo_c:hi_c]
        carry = jnp.zeros((tq, 1), F32)
        acc = jnp.zeros((tq, HEAD_DIM), F32)
        off = pl.multiple_of(kd * tk, tk)
        row = lax.broadcasted_iota(jnp.int32, (tq, tk), 0) + sub * tq
        colk = lax.broadcasted_iota(jnp.int32, (tq, tk), 1)
        carry, acc = block(q, k_ref[pl.ds(off, tk), lo_c:hi_c], v_ref[pl.ds(off, tk), lo_c:hi_c],
                           tri_ref[...], colk < row, carry, acc)

        def body(t, ca):
            kb = kd - 1 - t
            o2 = pl.multiple_of(kb * tk, tk)
            return block(q, k_ref[pl.ds(o2, tk), lo_c:hi_c], v_ref[pl.ds(o2, tk), lo_c:hi_c],
                         tri_ref[...], None, ca[0], ca[1])

        carry, acc = lax.fori_loop(0, kd, body, (carry, acc))
        carry, acc = block(q, mk_ref[:, lo_c:hi_c], mv_ref[:, lo_c:hi_c], trim_ref[...], None, carry, acc)
        outs.append(acc)
    o_ref[...] = jnp.concatenate(outs, axis=1).astype(o_ref.dtype)


def _suffix_matrix(n):
    j = lax.broadcasted_iota(jnp.int32, (n, n), 0)
    s = lax.broadcasted_iota(jnp.int32, (n, n), 1)
    return (j > s).astype(BF16)


def _attn_b(pb, pb_meta, *, batch, seq, n_heads, col_q, col_k, col_v, tq, tk):
    w = 2 * HEAD_DIM
    nq = seq // tq
    body = functools.partial(_attn_b_body, tq=tq, tk=tk)
    return pl.pallas_call(
        body,
        grid=(batch, n_heads // 2, nq),
        in_specs=[
            pl.BlockSpec((tq, w), lambda b, h, i: (b * nq + i, col_q + h)),
            pl.BlockSpec((seq, w), lambda b, h, i: (b, col_k + h)),
            pl.BlockSpec((seq, w), lambda b, h, i: (b, col_v + h)),
            pl.BlockSpec((N_META, w), lambda b, h, i: (0, col_k + h)),
            pl.BlockSpec((N_META, w), lambda b, h, i: (0, col_v + h)),
            pl.BlockSpec((tk, tk), lambda b, h, i: (0, 0)),
            pl.BlockSpec((N_META, N_META), lambda b, h, i: (0, 0)),
        ],
        out_specs=pl.BlockSpec((tq, w), lambda b, h, i: (b * nq + i, h)),
        out_shape=jax.ShapeDtypeStruct((batch * seq, n_heads * HEAD_DIM), BF16),
        compiler_params=_cparams(("parallel", "parallel", "arbitrary")),
        name="attn_b",
    )(pb, pb, pb, pb_meta, pb_meta, _suffix_matrix(tk), _suffix_matrix(N_META))


N_SUB = 8
N_LANE = 128


def _diag_mask():
    sub = lax.broadcasted_iota(jnp.int32, (N_SUB, N_LANE), 0)
    lane = lax.broadcasted_iota(jnp.int32, (N_SUB, N_LANE), 1)
    return (lane // (N_LANE // N_SUB)) == sub


def _diag_to_cols(x, diag):
    row = jnp.sum(jnp.where(diag, x, 0.0), axis=0, keepdims=True)
    return jnp.transpose(jnp.broadcast_to(row, (N_LANE, N_LANE)))


def _suffix_excl(x):
    n = x.shape[0]
    zero = jnp.zeros(x.shape[1:], F32)
    if n <= 8:
        outs, run = [None] * n, zero
        for j in range(n - 1, -1, -1):
            outs[j] = run
            run = run + x[j]
        return jnp.stack(outs, axis=0), run
    ng = n // 8
    g = x.reshape(ng, 8, N_SUB, N_LANE)
    inner, run = [None] * 8, jnp.zeros((ng, N_SUB, N_LANE), F32)
    for j in range(7, -1, -1):
        inner[j] = run
        run = run + g[:, j]
    offs, tot = [None] * ng, zero
    for i in range(ng - 1, -1, -1):
        offs[i] = tot
        tot = tot + run[i]
    off = jnp.stack(offs, axis=0)
    later = jnp.stack([w + off for w in inner], axis=1)
    return later.reshape(n, N_SUB, N_LANE), tot


def _sample_block(k_a, v_a, k_b, v_b, q_a, q_b, ntok, mask_a, mask_b, diag, even_head, state):
    m_a, l_a, acc_a, carry_b, acc_b = state
    rows = ntok * N_SUB
    tn_dims = (((0,), (0,)), ((), ()))
    s = jnp.dot(k_a, q_a, preferred_element_type=F32).reshape(ntok, N_SUB, N_LANE)
    if mask_a is not None:
        s = jnp.where(mask_a, s, NEG_INF)
    m_old = m_a[...]
    m_new = jnp.maximum(m_old, jnp.max(s, axis=0))
    alpha = jnp.exp(m_old - m_new)
    p = jnp.where(diag, jnp.exp(s - m_new), 0.0)
    l_a[...] = alpha * l_a[...] + jnp.sum(p, axis=0)
    m_a[...] = m_new
    pv = lax.dot_general(p.reshape(rows, N_LANE).astype(BF16), v_a, tn_dims, preferred_element_type=F32)
    a_col = _diag_to_cols(alpha, diag)
    acc_a[...] = acc_a[...] * jnp.concatenate([a_col, a_col], axis=1) + pv
    z = jnp.dot(k_b, q_b, preferred_element_type=F32).reshape(ntok, N_SUB, N_LANE)
    ls, lm = _log_sigmoid_pair(z)
    keep = diag
    if mask_b is not None:
        lm = jnp.where(mask_b, lm, 0.0)
        keep = jnp.logical_and(mask_b, diag)
    later, total = _suffix_excl(lm)
    a = jnp.where(keep, jnp.exp(ls + later + carry_b[...]), 0.0).reshape(rows, N_LANE)
    stacked = jnp.concatenate([jnp.where(even_head, a, 0.0), jnp.where(even_head, 0.0, a)], axis=0)
    acc_b[...] += lax.dot_general(stacked.astype(BF16), v_b, tn_dims, preferred_element_type=F32)
    carry_b[...] += total


def _sample_attn_body(pt_ref, lam_ref, qa_ref, qb_ref, nka_ref, nva_ref, nkb_ref, nvb_ref, g_ref,
                      cak_ref, cav_ref, cbk_ref, cbv_ref, oa_ref, ob_ref,
                      m_a, l_a, acc_a, carry_b, acc_b, *, lam_init, page, n_new):
    p = pl.program_id(1)
    diag = _diag_mask()
    lane = lax.broadcasted_iota(jnp.int32, (1, N_LANE), 1)
    even_head = ((lane // 8) % 2) == 0
    state = (m_a, l_a, acc_a, carry_b, acc_b)
    q_a, q_b = qa_ref[0], qb_ref[0]

    @pl.when(p == 0)
    def _():
        m_a[...] = jnp.full(m_a.shape, NEG_INF, F32)
        l_a[...] = jnp.zeros(l_a.shape, F32)
        acc_a[...] = jnp.zeros(acc_a.shape, F32)
        carry_b[...] = jnp.zeros(carry_b.shape, F32)
        acc_b[...] = jnp.zeros(acc_b.shape, F32)
        tok = lax.broadcasted_iota(jnp.int32, (n_new, N_SUB, N_LANE), 0)
        qpos = lax.broadcasted_iota(jnp.int32, (n_new, N_SUB, N_LANE), 2) % 8
        _sample_block(nka_ref[0], nva_ref[0], nkb_ref[0], nvb_ref[0], q_a, q_b, n_new,
                      tok <= qpos, tok < qpos, diag, even_head, state)

    @pl.when(p > 0)
    def _():
        rows = page * N_SUB

        def halves(ref):
            h0 = ref[0, :, pl.ds(0, N_SUB, stride=2), :].reshape(rows, HEAD_DIM).astype(BF16)
            h1 = ref[0, :, pl.ds(1, N_SUB, stride=2), :].reshape(rows, HEAD_DIM).astype(BF16)
            return h0, h1

        k_a = jnp.concatenate(halves(cak_ref), axis=1)
        v_a = cav_ref[0].reshape(rows, 2 * HEAD_DIM).astype(BF16)
        k_b = jnp.concatenate(halves(cbk_ref), axis=1)
        v_b = jnp.concatenate(halves(cbv_ref), axis=0)
        _sample_block(k_a, v_a, k_b, v_b, q_a, q_b, page, None, None, diag, even_head, state)

    @pl.when(p == pl.num_programs(1) - 1)
    def _():
        l_col = _diag_to_cols(l_a[...], diag)
        o = acc_a[...] / jnp.concatenate([l_col, l_col], axis=1)
        lam = lam_ref[0, 0]
        g = g_ref[...]
        w = 2 * HEAD_DIM
        for h in range(N_SUB):
            d = o[h * 16:h * 16 + 8] - lam * o[h * 16 + 8:h * 16 + 16]
            oa_ref[0, :, h * w:(h + 1) * w] = _subln(d, g, lam_init).astype(oa_ref.dtype)
        ob = acc_b[...]
        for h in range(2 * N_SUB):
            ob_ref[0, :, h * HEAD_DIM:(h + 1) * HEAD_DIM] = ob[h * 8:(h + 1) * 8].astype(ob_ref.dtype)


def _sample_attn(page_table, lam, q_a, q_b, nk_a, nv_a, nk_b, nv_b, g, ck_a, cv_a, ck_b, cv_b, *, lam_init):
    n_seq, n_pages = page_table.shape
    page = ck_a.shape[1]
    n_new = nk_a.shape[1] // N_SUB
    assert q_a.shape[1:] == (2 * HEAD_DIM, N_LANE) and cv_a.shape[2:] == (N_SUB, 2 * HEAD_DIM)
    assert ck_a.shape[2:] == (2 * N_SUB, HEAD_DIM) and ck_b.shape[2:] == (2 * N_SUB, HEAD_DIM)

    def seq_map(s, p, pt):
        return (s, 0, 0)

    def page_map(s, p, pt):
        return (pt[s, n_pages - 1 - jnp.maximum(p - 1, 0)], 0, 0, 0)

    w = 2 * HEAD_DIM
    body = functools.partial(_sample_attn_body, lam_init=lam_init, page=page, n_new=n_new)
    grid_spec = pltpu.PrefetchScalarGridSpec(
        num_scalar_prefetch=1,
        grid=(n_seq, n_pages + 1),
        in_specs=[
            pl.BlockSpec(memory_space=pltpu.SMEM),
            pl.BlockSpec((1, w, N_LANE), seq_map),
            pl.BlockSpec((1, w, N_LANE), seq_map),
            pl.BlockSpec((1, n_new * N_SUB, w), seq_map),
            pl.BlockSpec((1, n_new * N_SUB, w), seq_map),
            pl.BlockSpec((1, n_new * N_SUB, w), seq_map),
            pl.BlockSpec((1, 2 * n_new * N_SUB, HEAD_DIM), seq_map),
            pl.BlockSpec((1, w), lambda s, p, pt: (0, 0)),
            pl.BlockSpec((1, page, 2 * N_SUB, HEAD_DIM), page_map),
            pl.BlockSpec((1, page, N_SUB, w), page_map),
            pl.BlockSpec((1, page, 2 * N_SUB, HEAD_DIM), page_map),
            pl.BlockSpec((1, page, 2 * N_SUB, HEAD_DIM), page_map),
        ],
        out_specs=[
            pl.BlockSpec((1, n_new, N_SUB * w), seq_map),
            pl.BlockSpec((1, n_new, 2 * N_SUB * HEAD_DIM), seq_map),
        ],
        scratch_shapes=[
            pltpu.VMEM((N_SUB, N_LANE), F32),
            pltpu.VMEM((N_SUB, N_LANE), F32),
            pltpu.VMEM((N_LANE, w), F32),
            pltpu.VMEM((N_SUB, N_LANE), F32),
            pltpu.VMEM((N_LANE, HEAD_DIM), F32),
        ],
    )
    return pl.pallas_call(
        body,
        grid_spec=grid_spec,
        out_shape=[
            jax.ShapeDtypeStruct((n_seq, n_new, N_SUB * w), BF16),
            jax.ShapeDtypeStruct((n_seq, n_new, 2 * N_SUB * HEAD_DIM), BF16),
        ],
        compiler_params=_cparams(("parallel", "arbitrary")),
        name="sample_attn",
    )(page_table, lam, q_a, q_b, nk_a, nv_a, nk_b, nv_b, g, ck_a, cv_a, ck_b, cv_b)


def _sample_operands(pbs, n_seq, n_new, width_a, width_b):
    qa, ka, va, qb, kb, vb = jnp.split(
        pbs, [width_a, 2 * width_a, 3 * width_a, 3 * width_a + width_b, 3 * width_a + 2 * width_b], axis=1)
    eye2 = jnp.eye(2, dtype=pbs.dtype)
    qa5 = qa.reshape(n_seq, n_new, N_SUB, 2, HEAD_DIM).transpose(0, 3, 4, 2, 1)
    q_a = (qa5[:, :, :, :, None, :] * eye2[None, :, None, None, :, None]).reshape(n_seq, 2 * HEAD_DIM, N_LANE)
    qb4 = qb.reshape(n_seq, n_new, 2 * N_SUB, HEAD_DIM).transpose(0, 3, 2, 1)
    par = (jnp.arange(2 * N_SUB)[None, :] % 2 == jnp.arange(2)[:, None]).astype(pbs.dtype)
    q_b = (qb4[:, None] * par[None, :, None, :, None]).reshape(n_seq, 2 * HEAD_DIM, N_LANE)
    nk_a = ka.reshape(n_seq, n_new * N_SUB, 2 * HEAD_DIM)
    nv_a = va.reshape(n_seq, n_new * N_SUB, 2 * HEAD_DIM)
    nk_b = kb.reshape(n_seq, n_new * N_SUB, 2 * HEAD_DIM)
    nv_b = vb.reshape(n_seq, n_new, N_SUB, 2, HEAD_DIM).transpose(0, 3, 1, 2, 4).reshape(
        n_seq, 2 * n_new * N_SUB, HEAD_DIM)
    return q_a, q_b, nk_a, nv_a, nk_b, nv_b


def _outproj_body(oap_ref, obp_ref, xp_ref, oas_ref, obs_ref, xs_ref, wa_ref, wb_ref, o_ref, *, n_p):
    i = pl.program_id(0)

    def run(oa_ref, ob_ref, x_ref):
        acc = jnp.dot(oa_ref[...], wa_ref[...], preferred_element_type=F32)
        acc = acc + jnp.dot(ob_ref[...], wb_ref[...], preferred_element_type=F32)
        o_ref[...] = x_ref[...] + acc

    @pl.when(i < n_p)
    def _():
        run(oap_ref, obp_ref, xp_ref)

    @pl.when(i >= n_p)
    def _():
        run(oas_ref, obs_ref, xs_ref)


def _outproj(oa_p, ob_p, x_p, oa_s, ob_s, x_s, w_bf, *, tm, tn):
    rp, wa = oa_p.shape
    wb = ob_p.shape[1]
    d = w_bf.shape[1]
    assert rp % tm == 0 and oa_s.shape[0] == tm and wa == wb and d % tn == 0
    n_p = rp // tm

    def pmap(i, j):
        return (jnp.minimum(i, n_p - 1), 0)

    def pmap_x(i, j):
        return (jnp.minimum(i, n_p - 1), j)

    return pl.pallas_call(
        functools.partial(_outproj_body, n_p=n_p),
        grid=(n_p + 1, d // tn),
        in_specs=[
            pl.BlockSpec((tm, wa), pmap),
            pl.BlockSpec((tm, wb), pmap),
            pl.BlockSpec((tm, tn), pmap_x),
            pl.BlockSpec((tm, wa), lambda i, j: (0, 0)),
            pl.BlockSpec((tm, wb), lambda i, j: (0, 0)),
            pl.BlockSpec((tm, tn), lambda i, j: (0, j)),
            pl.BlockSpec((wa, tn), lambda i, j: (0, j)),
            pl.BlockSpec((wb, tn), lambda i, j: (1, j)),
        ],
        out_specs=pl.BlockSpec((tm, tn), lambda i, j: (i, j)),
        out_shape=jax.ShapeDtypeStruct((rp + tm, d), F32),
        compiler_params=_cparams(("parallel", "arbitrary")),
        name="outproj",
    )(oa_p, ob_p, x_p, oa_s, ob_s, x_s, w_bf, w_bf)


def _rmsnorm_f32(x, g):
    ms = jnp.mean(x * x, axis=-1, keepdims=True)
    return x * lax.rsqrt(ms + NORM_EPS) * g


def _first_lane(cond, lane_f):
    return jnp.min(jnp.where(cond, lane_f, float(N_LANE)), axis=1, keepdims=True)


def _router_body(h_ref, g_ref, whi_ref, wlo_ref, ids_ref, wts_ref):
    n = _rmsnorm_f32(h_ref[...], g_ref[...])
    hi, lo = _split_bf16(n)
    whi = whi_ref[...]
    logits = (jnp.dot(hi, whi, preferred_element_type=F32) + jnp.dot(lo, whi, preferred_element_type=F32)
              + jnp.dot(hi, wlo_ref[...], preferred_element_type=F32))
    lane = lax.broadcasted_iota(jnp.int32, logits.shape, 1)
    lane_f = lane.astype(F32)
    is_g = lane < N_GROUPS
    gl = jnp.where(is_g, logits, NEG_INF)
    gmax = jnp.max(gl, axis=1, keepdims=True)
    gsum = jnp.sum(jnp.exp(gl - gmax), axis=1, keepdims=True)
    g_gate = 1.0 / gsum
    g_idx = _first_lane(jnp.logical_and(is_g, gl == gmax), lane_f)
    e_lo = N_GROUPS + EXPERTS_PER_GROUP * g_idx
    sel = jnp.logical_and(lane_f >= e_lo, lane_f < e_lo + EXPERTS_PER_GROUP)
    el = jnp.where(sel, logits, NEG_INF)
    emax = jnp.max(el, axis=1, keepdims=True)
    ee = jnp.exp(el - emax)
    pe = ee / jnp.sum(ee, axis=1, keepdims=True)
    p1 = jnp.max(jnp.where(sel, pe, -1.0), axis=1, keepdims=True)
    i1 = _first_lane(jnp.logical_and(sel, pe == p1), lane_f)
    rest = jnp.logical_and(sel, lane_f != i1)
    p2 = jnp.max(jnp.where(rest, pe, -1.0), axis=1, keepdims=True)
    i2 = _first_lane(jnp.logical_and(rest, pe == p2), lane_f)
    denom = p1 + p2
    w1 = g_gate * p1 / denom
    w2 = g_gate * p2 / denom
    ids_ref[...] = jnp.where(lane == 0, i1 - N_GROUPS, jnp.where(lane == 1, i2 - N_GROUPS, 0.0)).astype(jnp.int32)
    wts_ref[...] = jnp.where(lane == 0, w1, jnp.where(lane == 1, w2, 0.0))


def _router(h, n_rows, g, w_hi, w_lo, *, tm):
    d = h.shape[1]
    assert n_rows % tm == 0
    return pl.pallas_call(
        _router_body,
        grid=(n_rows // tm,),
        in_specs=[
            pl.BlockSpec((tm, d), lambda i: (i, 0)),
            pl.BlockSpec((1, d), lambda i: (0, 0)),
            pl.BlockSpec((d, N_LANE), lambda i: (0, 0)),
            pl.BlockSpec((d, N_LANE), lambda i: (0, 0)),
        ],
        out_specs=[
            pl.BlockSpec((tm, N_LANE), lambda i: (i, 0)),
            pl.BlockSpec((tm, N_LANE), lambda i: (i, 0)),
        ],
        out_shape=[
            jax.ShapeDtypeStruct((n_rows, N_LANE), jnp.int32),
            jax.ShapeDtypeStruct((n_rows, N_LANE), F32),
        ],
        compiler_params=_cparams(("parallel",)),
        name="router",
    )(h, g, w_hi, w_lo)


def _dispatch_plan(ids, *, tm, n_split):
    t = ids.shape[0]
    e_flat = ids.reshape(-1)
    n_pairs = e_flat.shape[0]
    nt = (n_pairs + N_EXPERTS * (tm - 1)) // tm
    onehot = (e_flat[:, None] == jnp.arange(N_EXPERTS)[None, :]).astype(jnp.int32)
    counts = jnp.sum(onehot, axis=0)
    rank = jnp.sum((jnp.cumsum(onehot, axis=0) - 1) * onehot, axis=1)
    ntile = (counts + tm - 1) // tm
    tile_end = jnp.cumsum(ntile)
    tile_start = tile_end - ntile
    n_active = tile_end[-1]
    dest = tile_start[e_flat] * tm + rank
    order = jnp.argsort(e_flat, stable=True)
    pair_start = jnp.cumsum(counts) - counts
    tiles = jnp.arange(nt)
    tile_expert = jnp.minimum(jnp.searchsorted(tile_end, tiles, side="right"), N_EXPERTS - 1)
    slots = jnp.arange(nt * tm)
    s_exp = tile_expert[slots // tm]
    s_rank = slots - tile_start[s_exp] * tm
    s_valid = jnp.logical_and(slots // tm < n_active, s_rank < counts[s_exp])
    s_pair = order[jnp.clip(pair_start[s_exp] + s_rank, 0, n_pairs - 1)]
    slot_token = jnp.where(s_valid, s_pair // 2, 0).astype(jnp.int32).reshape(nt, tm)
    n_items = nt * n_split
    n_act_items = n_active * n_split
    item_end = tile_end * n_split
    idx = jnp.arange(n_items)
    live = idx < n_act_items
    w = jnp.minimum(idx, n_act_items - 1)
    w_exp = jnp.minimum(jnp.searchsorted(item_end, w, side="right"), N_EXPERTS - 1)
    local = w - tile_start[w_exp] * n_split
    w_nt = jnp.maximum(ntile[w_exp], 1)
    w_split = local // w_nt
    idle = idx - n_act_items
    o_tile = jnp.where(live, tile_start[w_exp] + local % w_nt, n_active + idle // n_split)
    o_split = jnp.where(live, w_split, idle % n_split)
    w_first = jnp.logical_and(local % w_nt == 0, live)
    items = jnp.stack([w_exp, w_split, o_tile, w_first.astype(jnp.int32), o_split], axis=0).astype(jnp.int32)
    counts_i = jnp.stack([n_active, n_active * n_split]).astype(jnp.int32)
    return dest.reshape(t, 2).astype(jnp.int32), slot_token, items, counts_i, nt


def _gather_norm_body(cnt_ref, tok_ref, g_ref, h_hbm, o_ref, buf, sem, *, tm):
    i = pl.program_id(0)

    @pl.when(i < cnt_ref[0])
    def _():
        def row_copy(r):
            return pltpu.make_async_copy(h_hbm.at[pl.ds(tok_ref[0, 0, r], 1)], buf.at[pl.ds(r, 1)], sem)

        def start(r, c):
            row_copy(r).start()
            return c

        def wait(r, c):
            row_copy(r).wait()
            return c

        lax.fori_loop(0, tm, start, 0)
        lax.fori_loop(0, tm, wait, 0)
        o_ref[...] = _rmsnorm_f32(buf[...], g_ref[...]).astype(o_ref.dtype)

    @pl.when(i >= cnt_ref[0])
    def _():
        o_ref[...] = jnp.zeros(o_ref.shape, o_ref.dtype)


def _gather_norm(counts_i, slot_token, g, h, *, tm):
    nt = slot_token.shape[0]
    assert slot_token.shape == (nt, 1, tm)
    d = h.shape[1]
    grid_spec = pltpu.PrefetchScalarGridSpec(
        num_scalar_prefetch=1,
        grid=(nt,),
        in_specs=[
            pl.BlockSpec((1, 1, tm), lambda i, c: (i, 0, 0), memory_space=pltpu.SMEM),
            pl.BlockSpec((1, d), lambda i, c: (0, 0)),
            pl.BlockSpec(memory_space=pl.ANY),
        ],
        out_specs=pl.BlockSpec((tm, d), lambda i, c: (i, 0)),
        scratch_shapes=[pltpu.VMEM((tm, d), F32), pltpu.SemaphoreType.DMA(())],
    )
    return pl.pallas_call(
        functools.partial(_gather_norm_body, tm=tm),
        grid_spec=grid_spec,
        out_shape=jax.ShapeDtypeStruct((nt * tm, d), BF16),
        compiler_params=_cparams(("arbitrary",)),
        name="gather_norm",
    )(counts_i, slot_token, g, h)


def _expert_up_body(items_ref, cnt_ref, x_ref, wg_ref, wu_ref, o_ref, wg_s, wu_s):
    w = pl.program_id(0)

    @pl.when(items_ref[3, w] == 1)
    def _():
        wg_s[...] = wg_ref[0].astype(BF16)
        wu_s[...] = wu_ref[0].astype(BF16)

    @pl.when(w < cnt_ref[1])
    def _():
        x = x_ref[...]
        a = jnp.dot(x, wg_s[...], preferred_element_type=F32)
        u = jnp.dot(x, wu_s[...], preferred_element_type=F32)
        o_ref[...] = (a * (1.0 / (1.0 + jnp.exp(-a))) * u).astype(o_ref.dtype)

    @pl.when(w >= cnt_ref[1])
    def _():
        o_ref[...] = jnp.zeros(o_ref.shape, o_ref.dtype)


def _expert_up(items, counts_i, xs, w_gate, w_up, *, tm, tf):
    n_items = items.shape[1]
    d, ff = w_gate.shape[1:]
    grid_spec = pltpu.PrefetchScalarGridSpec(
        num_scalar_prefetch=2,
        grid=(n_items,),
        in_specs=[
            pl.BlockSpec((tm, d), lambda w, it, c: (it[2, w], 0)),
            pl.BlockSpec((1, d, tf), lambda w, it, c: (it[0, w], 0, it[1, w])),
            pl.BlockSpec((1, d, tf), lambda w, it, c: (it[0, w], 0, it[1, w])),
        ],
        out_specs=pl.BlockSpec((tm, tf), lambda w, it, c: (it[2, w], it[4, w])),
        scratch_shapes=[pltpu.VMEM((d, tf), BF16), pltpu.VMEM((d, tf), BF16)],
    )
    return pl.pallas_call(
        _expert_up_body,
        grid_spec=grid_spec,
        out_shape=jax.ShapeDtypeStruct((xs.shape[0], ff), BF16),
        compiler_params=_cparams(("arbitrary",)),
        name="expert_up",
    )(items, counts_i, xs, w_gate, w_up)


def _expert_down_body(items_ref, cnt_ref, a_ref, wd_ref, o_ref, wd_s):
    w = pl.program_id(0)

    @pl.when(items_ref[3, w] == 1)
    def _():
        wd_s[...] = wd_ref[0].astype(BF16)

    @pl.when(w < cnt_ref[1])
    def _():
        o_ref[...] = jnp.dot(a_ref[...], wd_s[...], preferred_element_type=F32)

    @pl.when(w >= cnt_ref[1])
    def _():
        o_ref[...] = jnp.zeros(o_ref.shape, o_ref.dtype)


def _expert_down(items, counts_i, act, w_down, *, tm, tn):
    n_items = items.shape[1]
    ff, d = w_down.shape[1:]
    grid_spec = pltpu.PrefetchScalarGridSpec(
        num_scalar_prefetch=2,
        grid=(n_items,),
        in_specs=[
            pl.BlockSpec((tm, ff), lambda w, it, c: (it[2, w], 0)),
            pl.BlockSpec((1, ff, tn), lambda w, it, c: (it[0, w], 0, it[1, w])),
        ],
        out_specs=pl.BlockSpec((tm, tn), lambda w, it, c: (it[2, w], it[4, w])),
        scratch_shapes=[pltpu.VMEM((ff, tn), BF16)],
    )
    return pl.pallas_call(
        _expert_down_body,
        grid_spec=grid_spec,
        out_shape=jax.ShapeDtypeStruct((act.shape[0], d), F32),
        compiler_params=_cparams(("arbitrary",)),
        name="expert_down",
    )(items, counts_i, act, w_down)


def _combine_body(dest_ref, h_ref, wts_ref, g_ref, y_hbm, op_ref, os_ref, buf, sem, *, tm, n_p):
    i = pl.program_id(0)

    def row_copy(r, k):
        return pltpu.make_async_copy(y_hbm.at[pl.ds(dest_ref[0, 0, 2 * r + k], 1)], buf.at[k, pl.ds(r, 1)], sem)

    def start(r, c):
        row_copy(r, 0).start()
        row_copy(r, 1).start()
        return c

    def wait(r, c):
        row_copy(r, 0).wait()
        row_copy(r, 1).wait()
        return c

    lax.fori_loop(0, tm, start, 0)
    lax.fori_loop(0, tm, wait, 0)
    wts = wts_ref[...]
    out = h_ref[...] + (wts[:, 0:1] * buf[0] + wts[:, 1:2] * buf[1])
    y = _rmsnorm_f32(out, g_ref[...])

    @pl.when(i < n_p)
    def _():
        op_ref[...] = y

    @pl.when(i >= n_p)
    def _():
        os_ref[...] = y


def _combine(dest, h, wts, g, y, *, tm, n_p, n_s):
    d = h.shape[1]
    grid_spec = pltpu.PrefetchScalarGridSpec(
        num_scalar_prefetch=0,
        grid=(n_p + n_s,),
        in_specs=[
            pl.BlockSpec((1, 1, 2 * tm), lambda i: (i, 0, 0), memory_space=pltpu.SMEM),
            pl.BlockSpec((tm, d), lambda i: (i, 0)),
            pl.BlockSpec((tm, N_LANE), lambda i: (i, 0)),
            pl.BlockSpec((1, d), lambda i: (0, 0)),
            pl.BlockSpec(memory_space=pl.ANY),
        ],
        out_specs=[
            pl.BlockSpec((tm, d), lambda i: (jnp.minimum(i, n_p - 1), 0)),
            pl.BlockSpec((tm, d), lambda i: (jnp.maximum(i - n_p, 0), 0)),
        ],
        scratch_shapes=[pltpu.VMEM((2, tm, d), F32), pltpu.SemaphoreType.DMA(())],
    )
    return pl.pallas_call(
        functools.partial(_combine_body, tm=tm, n_p=n_p),
        grid_spec=grid_spec,
        out_shape=[
            jax.ShapeDtypeStruct((n_p * tm, d), F32),
            jax.ShapeDtypeStruct((n_s * tm, d), F32),
        ],
        compiler_params=_cparams(("arbitrary",)),
        name="combine",
    )(dest, h, wts, g, y)


def kernel(x_prompt, x_sample, cache_a_k, cache_a_v, cache_b_k, cache_b_v, page_table, meta_tokens, attn_norm, w_in, lam_q1, lam_k1, lam_q2, lam_k2, subln_norm, w_out, ffn_norm, w_router_group, w_router_expert, w_gate, w_up, w_down, final_norm):
    assert w_in.shape[0] == 1, "single-layer step"
    layer = 0
    b, seq, d = x_prompt.shape
    n_seq, n_new, _ = x_sample.shape
    n_pool, page = cache_a_k.shape[1], cache_a_k.shape[2]
    n_heads_a, n_heads_b = cache_a_v.shape[3], cache_b_k.shape[3]
    width_a, width_b = n_heads_a * 2 * HEAD_DIM, n_heads_b * HEAD_DIM
    past_len = page_table.shape[1] * page
    rows_p, rows_s = b * seq, n_seq * n_new
    wblk = 2 * HEAD_DIM

    lam_init = 0.8 - 0.6 * math.exp(-0.3 * layer)
    lam = (jnp.exp(jnp.sum(lam_q1[layer].astype(F32) * lam_k1[layer].astype(F32)))
           - jnp.exp(jnp.sum(lam_q2[layer].astype(F32) * lam_k2[layer].astype(F32))) + lam_init).reshape(1, 1)
    g_attn = attn_norm[layer].reshape(1, d)
    g_subln = subln_norm[layer].reshape(1, wblk)
    g_ffn = ffn_norm[layer].reshape(1, d)

    w_in_bf = w_in[layer].astype(BF16)
    xp = x_prompt.reshape(rows_p, d)
    cos_p, sin_p = _rope_tables(jnp.tile(N_META + jnp.arange(seq), b))
    pb, kv = _inproj(xp, g_attn, w_in_bf, cos_p, sin_p, tm=512, tn=512, width_a=width_a, width_b=width_b)
    x_small = jnp.concatenate([x_sample.reshape(rows_s, d), meta_tokens.astype(F32)], axis=0)
    pos_small = jnp.concatenate([past_len + jnp.tile(jnp.arange(n_new), n_seq), jnp.arange(N_META)])
    cos_s, sin_s = _rope_tables(pos_small)
    pbs, kvs = _inproj(x_small, g_attn, w_in_bf, cos_s, sin_s, tm=rows_s + N_META, tn=512,
                       width_a=width_a, width_b=width_b)
    pb_meta = pbs[rows_s:]

    ca, cb = width_a // wblk, width_b // wblk
    oa_p = _attn_a(lam, pb, pb_meta, g_subln, batch=b, seq=seq, n_heads=n_heads_a,
                   col_q=0, col_k=ca, col_v=2 * ca, tq=256, lam_init=lam_init)
    ob_p = _attn_b(pb, pb_meta, batch=b, seq=seq, n_heads=n_heads_b,
                   col_q=3 * ca, col_k=3 * ca + cb, col_v=3 * ca + 2 * cb, tq=128, tk=256)
    ops = _sample_operands(pbs[:rows_s], n_seq, n_new, width_a, width_b)
    oa_s, ob_s = _sample_attn(
        page_table, lam, *ops, g_subln,
        cache_a_k[layer].reshape(n_pool, page, 2 * n_heads_a, HEAD_DIM),
        cache_a_v[layer].reshape(n_pool, page, n_heads_a, wblk),
        cache_b_k[layer].reshape(n_pool, page, n_heads_b, HEAD_DIM),
        cache_b_v[layer].reshape(n_pool, page, n_heads_b, HEAD_DIM), lam_init=lam_init)

    tm_o = 512
    pad_s = lambda a: jnp.pad(a, ((0, tm_o - rows_s), (0, 0)))
    h = _outproj(oa_p, ob_p, xp, pad_s(oa_s.reshape(rows_s, width_a)), pad_s(ob_s.reshape(rows_s, width_b)),
                 pad_s(x_sample.reshape(rows_s, d)), w_out[layer].astype(BF16), tm=tm_o, tn=1024)

    tm_e = 256
    n_tok = rows_p + rows_s
    w_r = jnp.concatenate([w_router_group[layer], w_router_expert[layer].reshape(d, N_EXPERTS)], axis=1).astype(F32)
    w_r = jnp.pad(w_r, ((0, 0), (0, N_LANE - w_r.shape[1])))
    w_r_hi, w_r_lo = _split_bf16(w_r)
    ids, wts = _router(h, n_tok, g_ffn, w_r_hi, w_r_lo, tm=tm_e)
    n_split = 4
    dest, slot_token, items, counts_i, nt = _dispatch_plan(ids[:, :2], tm=tm_e, n_split=n_split)
    xs_sorted = _gather_norm(counts_i, slot_token.reshape(nt, 1, tm_e), g_ffn, h, tm=tm_e)
    act = _expert_up(items, counts_i, xs_sorted, w_gate[layer], w_up[layer], tm=tm_e, tf=w_gate.shape[3] // n_split)
    y = _expert_down(items, counts_i, act, w_down[layer], tm=tm_e, tn=d // n_split)
    y_p, y_s = _combine(dest.reshape(n_tok // tm_e, 1, 2 * tm_e), h, wts, final_norm.reshape(1, d), y,
                        tm=tm_e, n_p=rows_p // tm_e, n_s=rows_s // tm_e)

    def prompt_cache(lo, width, shape):
        meta = jnp.broadcast_to(kvs[rows_s:, lo:lo + width][None], (b, N_META, width))
        full = jnp.concatenate([meta, kv[:, lo:lo + width].reshape(b, seq, width)], axis=1)
        return full.reshape((1, b, seq + N_META) + shape)

    def sample_cache(lo, width, shape):
        return kvs[:rows_s, lo:lo + width].reshape((1, n_seq, n_new) + shape)

    shapes = [(0, width_a, (n_heads_a, 2, HEAD_DIM)), (width_a, width_a, (n_heads_a, wblk)),
              (2 * width_a, width_b, (n_heads_b, HEAD_DIM)), (2 * width_a + width_b, width_b, (n_heads_b, HEAD_DIM))]
    return ((y_p.reshape(b, seq, d), y_s.reshape(n_seq, n_new, d))
            + tuple(prompt_cache(*s) for s in shapes) + tuple(sample_cache(*s) for s in shapes))
```

```python
import functools
import math

import jax
import jax.numpy as jnp
from jax import lax
from jax.experimental import pallas as pl
from jax.experimental.pallas import tpu as pltpu

F32 = jnp.float32
BF16 = jnp.bfloat16

HEAD_DIM = 128
N_META = 16
ROPE_THETA = 10000.0
NORM_EPS = 1e-6
SUBLN_EPS = 1e-5
NEG_INF = -1e30
N_GROUPS = 4
EXPERTS_PER_GROUP = 8
N_EXPERTS = N_GROUPS * EXPERTS_PER_GROUP
LOG2E = math.log2(math.e)
QK_SCALE = HEAD_DIM ** -0.5 * LOG2E

VMEM_LIMIT_BYTES = 56 * 1024 * 1024


def _cparams(sem):
    return pltpu.CompilerParams(dimension_semantics=sem, vmem_limit_bytes=VMEM_LIMIT_BYTES)


def _inproj_body(x_ref, g_ref, w_ref, cos_ref, sin_ref, pb_ref, kv_ref, n_scr, *, tn, width_a, width_b):
    j = pl.program_id(1)

    @pl.when(j == 0)
    def _():
        x = x_ref[...]
        ms = jnp.mean(x * x, axis=-1, keepdims=True)
        n_scr[...] = (x * lax.rsqrt(ms + NORM_EPS) * g_ref[...]).astype(BF16)

    acc = jnp.dot(n_scr[...], w_ref[...], preferred_element_type=F32)
    col = j * tn
    is_rope = col < 2 * width_a
    is_q = (col < width_a) | ((col >= 3 * width_a) & (col < 3 * width_a + width_b))
    is_kv = jnp.logical_not(is_q)

    def emit(val):
        sc = jnp.where(is_q, QK_SCALE, 1.0).astype(F32)
        pb_ref[...] = (val * sc).astype(BF16)

        @pl.when(is_kv)
        def _():
            kv_ref[...] = val

    @pl.when(is_rope)
    def _():
        cos = cos_ref[...]
        sin = sin_ref[...]
        chunks = []
        for c in range(tn // HEAD_DIM):
            a = acc[:, c * HEAD_DIM:(c + 1) * HEAD_DIM]
            chunks.append(a * cos + pltpu.roll(a, HEAD_DIM // 2, 1) * sin)
        emit(jnp.concatenate(chunks, axis=1))

    @pl.when(jnp.logical_not(is_rope))
    def _():
        emit(acc)


def _inproj(x, g, w_bf, cos2, sin2, *, tm, tn, width_a, width_b):
    r, d = x.shape
    d_in = w_bf.shape[1]
    assert r % tm == 0 and d_in % tn == 0 and width_a % tn == 0 and width_b % tn == 0
    na, nb = width_a // tn, width_b // tn

    def kv_map(i, j):
        ja = jnp.clip(j - na, 0, 2 * na - 1)
        jb = jnp.clip(j - (3 * na + nb), 0, 2 * nb - 1) + 2 * na
        return (i, jnp.where(j < 3 * na, ja, jb))

    body = functools.partial(_inproj_body, tn=tn, width_a=width_a, width_b=width_b)
    return pl.pallas_call(
        body,
        grid=(r // tm, d_in // tn),
        in_specs=[
            pl.BlockSpec((tm, d), lambda i, j: (i, 0)),
            pl.BlockSpec((1, d), lambda i, j: (0, 0)),
            pl.BlockSpec((d, tn), lambda i, j: (0, j)),
            pl.BlockSpec((tm, HEAD_DIM), lambda i, j: (i, 0)),
            pl.BlockSpec((tm, HEAD_DIM), lambda i, j: (i, 0)),
        ],
        out_specs=[
            pl.BlockSpec((tm, tn), lambda i, j: (i, j)),
            pl.BlockSpec((tm, tn), kv_map),
        ],
        out_shape=[
            jax.ShapeDtypeStruct((r, d_in), BF16),
            jax.ShapeDtypeStruct((r, 2 * width_a + 2 * width_b), F32),
        ],
        scratch_shapes=[pltpu.VMEM((tm, d), BF16)],
        compiler_params=_cparams(("parallel", "arbitrary")),
        name="inproj",
    )(x, g, w_bf, cos2, sin2)


def _rope_tables(pos):
    half = HEAD_DIM // 2
    inv = ROPE_THETA ** (-jnp.arange(half, dtype=F32) / half)
    ang = pos.astype(F32)[:, None] * inv[None, :]
    cos, sin = jnp.cos(ang), jnp.sin(ang)
    return jnp.concatenate([cos, cos], axis=1), jnp.concatenate([-sin, sin], axis=1)


def _subln(d, g, lam_init):
    ms = jnp.mean(d * d, axis=-1, keepdims=True)
    return d * lax.rsqrt(ms + SUBLN_EPS) * g * (1.0 - lam_init)


def _split_bf16(x):
    hi = x.astype(BF16)
    lo = (x - hi.astype(F32)).astype(BF16)
    return hi, lo


KB = 256
QG = 256
NT_DIMS = (((1,), (1,)), ((), ()))


def _load_vt(v_ref, vt_scr):
    for c in range(vt_scr.shape[0]):
        vt_scr[c] = jnp.transpose(v_ref[c * KB:(c + 1) * KB, :].astype(F32)).astype(BF16)


def _attn_a_body(lam_ref, q_ref, k_ref, v_ref, mk_ref, mvt_ref, g_ref, o_ref, vt_scr, m_scr, l_scr, acc_scr,
                  *, tq, lam_init):
    qi = pl.program_id(2)

    @pl.when(qi == 0)
    def _():
        _load_vt(v_ref, vt_scr)

    n_hf = tq // QG
    chains = [(c, hf) for c in range(2) for hf in range(n_hf)]

    def scores(c, hf, k_blk):
        return lax.dot_general(k_blk[:, c * HEAD_DIM:(c + 1) * HEAD_DIM],
                               q_ref[hf * QG:(hf + 1) * QG, c * HEAD_DIM:(c + 1) * HEAD_DIM],
                               NT_DIMS, preferred_element_type=F32)

    off_d = pl.multiple_of(qi * KB, KB)
    kr = lax.broadcasted_iota(jnp.int32, (KB, QG), 0)
    qc = lax.broadcasted_iota(jnp.int32, (KB, QG), 1)
    for c, hf in chains:
        i = c * n_hf + hf
        s_d = jnp.where(kr <= qc + hf * QG, scores(c, hf, k_ref[pl.ds(off_d, KB), :]), NEG_INF)
        s_m = scores(c, hf, mk_ref[...])
        m = jnp.maximum(jnp.max(s_d, axis=0, keepdims=True), jnp.max(s_m, axis=0, keepdims=True))
        p_d = jnp.exp2(s_d - m)
        p_m = jnp.exp2(s_m - m)
        l_scr[i] = jnp.sum(p_d, axis=0, keepdims=True) + jnp.sum(p_m, axis=0, keepdims=True)
        acc_scr[i] = (jnp.dot(vt_scr[qi], p_d.astype(BF16), preferred_element_type=F32)
                      + jnp.dot(mvt_ref[...], p_m.astype(BF16), preferred_element_type=F32))
        m_scr[i] = m

    def full_blocks(first, n):
        off = pl.multiple_of(first * KB, KB)
        k_blk = k_ref[pl.ds(off, n * KB), :]
        ss = [scores(c, hf, k_blk) for c, hf in chains]
        alphas, ps = [], []
        for i, s in enumerate(ss):
            m_prev = m_scr[i]
            m_new = jnp.maximum(m_prev, jnp.max(s, axis=0, keepdims=True))
            alpha = jnp.exp2(m_prev - m_new)
            p32 = jnp.exp2(s - m_new)
            l_scr[i] = alpha * l_scr[i] + jnp.sum(p32, axis=0, keepdims=True)
            m_scr[i] = m_new
            alphas.append(alpha)
            ps.append(p32.astype(BF16))
        pvs = []
        for p in ps:
            pv = jnp.dot(vt_scr[first], p[:KB], preferred_element_type=F32)
            for t in range(1, n):
                pv = pv + jnp.dot(vt_scr[first + t], p[t * KB:(t + 1) * KB], preferred_element_type=F32)
            pvs.append(pv)
        for i in range(len(chains)):
            acc_scr[i] = alphas[i] * acc_scr[i] + pvs[i]

    @pl.when(qi % 2 == 1)
    def _():
        full_blocks(qi - 1, 1)

    def pair(t, carry):
        full_blocks(2 * t, 2)
        return carry

    lax.fori_loop(0, qi // 2, pair, 0)

    lam = lam_ref[0, 0]
    g = g_ref[...]
    for hf in range(n_hf):
        d_t = acc_scr[hf] / l_scr[hf] - lam * (acc_scr[n_hf + hf] / l_scr[n_hf + hf])
        o_ref[hf * QG:(hf + 1) * QG, :] = _subln(jnp.transpose(d_t), g, lam_init).astype(o_ref.dtype)


def _attn_a(lam, pb, pb_meta, pbt_meta, g, *, batch, seq, n_heads, col_q, col_k, col_v, lam_init):
    w = 2 * HEAD_DIM
    tq = KB
    nq = seq // tq
    return pl.pallas_call(
        functools.partial(_attn_a_body, tq=tq, lam_init=lam_init),
        grid=(batch, n_heads, nq),
        in_specs=[
            pl.BlockSpec(memory_space=pltpu.SMEM),
            pl.BlockSpec((tq, w), lambda b, h, i: (b * nq + i, col_q + h)),
            pl.BlockSpec((seq, w), lambda b, h, i: (b, col_k + h)),
            pl.BlockSpec((seq, w), lambda b, h, i: (b, col_v + h)),
            pl.BlockSpec((N_META, w), lambda b, h, i: (0, col_k + h)),
            pl.BlockSpec((w, N_META), lambda b, h, i: (col_v + h, 0)),
            pl.BlockSpec((1, w), lambda b, h, i: (0, 0)),
        ],
        out_specs=pl.BlockSpec((tq, w), lambda b, h, i: (b * nq + i, h)),
        out_shape=jax.ShapeDtypeStruct((batch * seq, n_heads * w), BF16),
        scratch_shapes=[
            pltpu.VMEM((seq // KB, w, KB), BF16),
            pltpu.VMEM((2 * tq // QG, 1, QG), F32),
            pltpu.VMEM((2 * tq // QG, 1, QG), F32),
            pltpu.VMEM((2 * tq // QG, w, QG), F32),
        ],
        compiler_params=_cparams(("parallel", "parallel", "arbitrary")),
        name="attn_a",
    )(lam, pb, pb, pb, pb_meta, pbt_meta, g)


def _neg_softplus2(z):
    return -jnp.maximum(z, 0.0) - jnp.log2(1.0 + jnp.exp2(-jnp.abs(z)))


def _attn_b_body(q_ref, k_ref, v_ref, mk_ref, mvt_ref, tri_ref, trim_ref, o_ref, vt_scr, carry_scr, acc_scr, *, tq):
    qi = pl.program_id(2)

    @pl.when(qi == 0)
    def _():
        _load_vt(v_ref, vt_scr)

    n_hf = tq // QG
    chains = [(hh, hf) for hh in range(2) for hf in range(n_hf)]

    n_ch = len(chains)

    def block(k_blk, vt_blk, tri, masks, first):
        zs, lms, parts = [], [], []
        for hh, hf in chains:
            zs.append(lax.dot_general(k_blk[:, hh * HEAD_DIM:(hh + 1) * HEAD_DIM],
                                      q_ref[hf * QG:(hf + 1) * QG, hh * HEAD_DIM:(hh + 1) * HEAD_DIM],
                                      NT_DIMS, preferred_element_type=F32))
        for i in range(n_ch):
            lm = _neg_softplus2(zs[i])
            if masks is not None:
                lm = jnp.where(masks[i], lm, 0.0)
            lms.append(lm)
            parts.extend(_split_bf16(lm))
        both = jnp.dot(tri, jnp.concatenate(parts, axis=1), preferred_element_type=F32)
        pvs = []
        for i, (hh, hf) in enumerate(chains):
            later = both[:, 2 * i * QG:(2 * i + 1) * QG] + both[:, (2 * i + 1) * QG:(2 * i + 2) * QG]
            e = lms[i] + zs[i] + later
            if not first:
                e = e + carry_scr[i]
            a = jnp.exp2(e)
            if masks is not None:
                a = jnp.where(masks[i], a, 0.0)
            pvs.append(jnp.dot(vt_blk[hh * HEAD_DIM:(hh + 1) * HEAD_DIM], a.astype(BF16),
                               preferred_element_type=F32))
        for i in range(n_ch):
            tot = jnp.sum(lms[i], axis=0, keepdims=True)
            if first:
                acc_scr[i] = pvs[i]
                carry_scr[i] = tot
            else:
                acc_scr[i] += pvs[i]
                carry_scr[i] += tot

    off_d = pl.multiple_of(qi * KB, KB)
    kr = lax.broadcasted_iota(jnp.int32, (KB, QG), 0)
    qc = lax.broadcasted_iota(jnp.int32, (KB, QG), 1)
    block(k_ref[pl.ds(off_d, KB), :], vt_scr[qi], tri_ref[...], [kr < qc + hf * QG for _, hf in chains], True)

    def body(t, carry):
        kb = qi - 1 - t
        off = pl.multiple_of(kb * KB, KB)
        block(k_ref[pl.ds(off, KB), :], vt_scr[kb], tri_ref[...], None, False)
        return carry

    lax.fori_loop(0, qi, body, 0)
    block(mk_ref[...], mvt_ref[...], trim_ref[...], None, False)
    for hf in range(n_hf):
        o_t = jnp.concatenate([acc_scr[hf], acc_scr[n_hf + hf]], axis=0)
        o_ref[hf * QG:(hf + 1) * QG, :] = jnp.transpose(o_t).astype(o_ref.dtype)


def _suffix_matrix_t(n):
    s = lax.broadcasted_iota(jnp.int32, (n, n), 0)
    j = lax.broadcasted_iota(jnp.int32, (n, n), 1)
    return (j > s).astype(BF16)


def _attn_b(pb, pb_meta, pbt_meta, *, batch, seq, n_heads, col_q, col_k, col_v):
    w = 2 * HEAD_DIM
    tq = KB
    nq = seq // tq
    return pl.pallas_call(
        functools.partial(_attn_b_body, tq=tq),
        grid=(batch, n_heads // 2, nq),
        in_specs=[
            pl.BlockSpec((tq, w), lambda b, h, i: (b * nq + i, col_q + h)),
            pl.BlockSpec((seq, w), lambda b, h, i: (b, col_k + h)),
            pl.BlockSpec((seq, w), lambda b, h, i: (b, col_v + h)),
            pl.BlockSpec((N_META, w), lambda b, h, i: (0, col_k + h)),
            pl.BlockSpec((w, N_META), lambda b, h, i: (col_v + h, 0)),
            pl.BlockSpec((KB, KB), lambda b, h, i: (0, 0)),
            pl.BlockSpec((N_META, N_META), lambda b, h, i: (0, 0)),
        ],
        out_specs=pl.BlockSpec((tq, w), lambda b, h, i: (b * nq + i, h)),
        out_shape=jax.ShapeDtypeStruct((batch * seq, n_heads * HEAD_DIM), BF16),
        scratch_shapes=[
            pltpu.VMEM((seq // KB, w, KB), BF16),
            pltpu.VMEM((2 * tq // QG, 1, QG), F32),
            pltpu.VMEM((2 * tq // QG, HEAD_DIM, QG), F32),
        ],
        compiler_params=_cparams(("parallel", "parallel", "arbitrary")),
        name="attn_b",
    )(pb, pb, pb, pb_meta, pbt_meta, _suffix_matrix_t(KB), _suffix_matrix_t(N_META))


SAMPLE_PAGES_PER_STEP = 2
N_SUB = 8
N_LANE = 128


def _diag_mask():
    sub = lax.broadcasted_iota(jnp.int32, (N_SUB, N_LANE), 0)
    lane = lax.broadcasted_iota(jnp.int32, (N_SUB, N_LANE), 1)
    return (lane // (N_LANE // N_SUB)) == sub


def _diag_to_cols(x, diag):
    row = jnp.sum(jnp.where(diag, x, 0.0), axis=0, keepdims=True)
    return jnp.transpose(jnp.broadcast_to(row, (N_LANE, N_LANE)))


def _suffix_excl(x):
    n = x.shape[0]
    zero = jnp.zeros(x.shape[1:], F32)
    if n <= 8:
        outs, run = [None] * n, zero
        for j in range(n - 1, -1, -1):
            outs[j] = run
            run = run + x[j]
        return jnp.stack(outs, axis=0), run
    ng = n // 8
    g = x.reshape(ng, 8, N_SUB, N_LANE)
    inner, run = [None] * 8, jnp.zeros((ng, N_SUB, N_LANE), F32)
    for j in range(7, -1, -1):
        inner[j] = run
        run = run + g[:, j]
    offs, tot = [None] * ng, zero
    for i in range(ng - 1, -1, -1):
        offs[i] = tot
        tot = tot + run[i]
    off = jnp.stack(offs, axis=0)
    later = jnp.stack([w + off for w in inner], axis=1)
    return later.reshape(n, N_SUB, N_LANE), tot


def _sample_block(k_a, v_a, k_b, v_b, q_a, q_b, ntok, mask_a, mask_b, diag, state):
    m_a, l_a, acc_a, carry_b, acc_b = state
    rows = ntok * N_SUB
    tn_dims = (((0,), (0,)), ((), ()))
    s = jnp.dot(k_a, q_a, preferred_element_type=F32).reshape(ntok, N_SUB, N_LANE)
    z = jnp.dot(k_b, q_b, preferred_element_type=F32).reshape(ntok, N_SUB, N_LANE)
    if mask_a is not None:
        s = jnp.where(mask_a, s, NEG_INF)
    m_old = m_a[...]
    m_new = jnp.maximum(m_old, jnp.max(s, axis=0))
    alpha = jnp.exp2(m_old - m_new)
    p = jnp.where(diag, jnp.exp2(s - m_new), 0.0)
    l_a[...] = alpha * l_a[...] + jnp.sum(p, axis=0)
    m_a[...] = m_new
    pv_a = lax.dot_general(p.reshape(rows, N_LANE).astype(BF16), v_a, tn_dims, preferred_element_type=F32)
    lm = _neg_softplus2(z)
    ls = lm + z
    keep = diag
    if mask_b is not None:
        lm = jnp.where(mask_b, lm, 0.0)
        keep = jnp.logical_and(mask_b, diag)
    later, total = _suffix_excl(lm)
    a = jnp.where(keep, jnp.exp2(ls + later + carry_b[...]), 0.0).reshape(rows, N_LANE)
    pv_b = lax.dot_general(a.astype(BF16), v_b, tn_dims, preferred_element_type=F32)
    a_col = _diag_to_cols(alpha, diag)
    acc_a[...] = acc_a[...] * jnp.concatenate([a_col, a_col], axis=1) + pv_a
    acc_b[...] += pv_b
    carry_b[...] += total


def _sample_attn_body(pt_ref, lam_ref, qa_ref, qb_ref, nka_ref, nva_ref, nkb_ref, nvb_ref, g_ref, *rest,
                      lam_init, page, n_new, pps):
    page_refs, (oa_ref, ob_ref, m_a, l_a, acc_a, carry_b, acc_b) = rest[:4 * pps], rest[4 * pps:]
    p = pl.program_id(1)
    diag = _diag_mask()
    state = (m_a, l_a, acc_a, carry_b, acc_b)
    q_a, q_b = qa_ref[0], qb_ref[0]

    @pl.when(p == 0)
    def _():
        m_a[...] = jnp.full(m_a.shape, NEG_INF, F32)
        l_a[...] = jnp.zeros(l_a.shape, F32)
        acc_a[...] = jnp.zeros(acc_a.shape, F32)
        carry_b[...] = jnp.zeros(carry_b.shape, F32)
        acc_b[...] = jnp.zeros(acc_b.shape, F32)
        tok = lax.broadcasted_iota(jnp.int32, (n_new, N_SUB, N_LANE), 0)
        qpos = lax.broadcasted_iota(jnp.int32, (n_new, N_SUB, N_LANE), 2) % 8
        _sample_block(nka_ref[0], nva_ref[0], nkb_ref[0], nvb_ref[0], q_a, q_b, n_new,
                      tok <= qpos, tok < qpos, diag, state)

    @pl.when(p > 0)
    def _():
        rows = page * N_SUB

        def halves(ref):
            h0 = ref[0, :, pl.ds(0, N_SUB, stride=2), :].reshape(rows, HEAD_DIM).astype(BF16)
            h1 = ref[0, :, pl.ds(1, N_SUB, stride=2), :].reshape(rows, HEAD_DIM).astype(BF16)
            return h0, h1

        for j in range(pps):
            cak_ref, cav_ref, cbk_ref, cbv_ref = page_refs[4 * j:4 * j + 4]
            k_a = jnp.concatenate(halves(cak_ref), axis=1)
            v_a = cav_ref[0].reshape(rows, 2 * HEAD_DIM).astype(BF16)
            k_b = jnp.concatenate(halves(cbk_ref), axis=1)
            v_b = jnp.concatenate(halves(cbv_ref), axis=1)
            _sample_block(k_a, v_a, k_b, v_b, q_a, q_b, page, None, None, diag, state)

    @pl.when(p == pl.num_programs(1) - 1)
    def _():
        l_col = _diag_to_cols(l_a[...], diag)
        o = acc_a[...] / jnp.concatenate([l_col, l_col], axis=1)
        lam = lam_ref[0, 0]
        g = g_ref[...]
        w = 2 * HEAD_DIM
        for h in range(N_SUB):
            d = o[h * 16:h * 16 + 8] - lam * o[h * 16 + 8:h * 16 + 16]
            oa_ref[0, :, h * w:(h + 1) * w] = _subln(d, g, lam_init).astype(oa_ref.dtype)
        ob = acc_b[...]
        for h in range(2 * N_SUB):
            par = (h % 2) * HEAD_DIM
            ob_ref[0, :, h * HEAD_DIM:(h + 1) * HEAD_DIM] = ob[h * 8:(h + 1) * 8, par:par + HEAD_DIM].astype(ob_ref.dtype)


def _sample_attn(page_table, lam, q_a, q_b, nk_a, nv_a, nk_b, nv_b, g, ck_a, cv_a, ck_b, cv_b, *, lam_init):
    n_seq, n_pages = page_table.shape
    page = ck_a.shape[1]
    n_new = nk_a.shape[1] // N_SUB
    assert q_a.shape[1:] == (2 * HEAD_DIM, N_LANE) and cv_a.shape[2:] == (N_SUB, 2 * HEAD_DIM)
    assert ck_a.shape[2:] == (2 * N_SUB, HEAD_DIM) and ck_b.shape[2:] == (2 * N_SUB, HEAD_DIM)

    def seq_map(s, p, pt):
        return (s, 0, 0)

    pps = SAMPLE_PAGES_PER_STEP
    assert n_pages % pps == 0

    def page_map(j):
        return lambda s, p, pt: (pt[s, n_pages - 1 - (jnp.maximum(p - 1, 0) * pps + j)], 0, 0, 0)

    w = 2 * HEAD_DIM
    page_specs = []
    for j in range(pps):
        page_specs += [
            pl.BlockSpec((1, page, 2 * N_SUB, HEAD_DIM), page_map(j)),
            pl.BlockSpec((1, page, N_SUB, w), page_map(j)),
            pl.BlockSpec((1, page, 2 * N_SUB, HEAD_DIM), page_map(j)),
            pl.BlockSpec((1, page, 2 * N_SUB, HEAD_DIM), page_map(j)),
        ]
    body = functools.partial(_sample_attn_body, lam_init=lam_init, page=page, n_new=n_new, pps=pps)
    grid_spec = pltpu.PrefetchScalarGridSpec(
        num_scalar_prefetch=1,
        grid=(n_seq, n_pages // pps + 1),
        in_specs=[
            pl.BlockSpec(memory_space=pltpu.SMEM),
            pl.BlockSpec((1, w, N_LANE), seq_map),
            pl.BlockSpec((1, w, N_LANE), seq_map),
            pl.BlockSpec((1, n_new * N_SUB, w), seq_map),
            pl.BlockSpec((1, n_new * N_SUB, w), seq_map),
            pl.BlockSpec((1, n_new * N_SUB, w), seq_map),
            pl.BlockSpec((1, n_new * N_SUB, w), seq_map),
            pl.BlockSpec((1, w), lambda s, p, pt: (0, 0)),
        ] + page_specs,
        out_specs=[
            pl.BlockSpec((1, n_new, N_SUB * w), seq_map),
            pl.BlockSpec((1, n_new, 2 * N_SUB * HEAD_DIM), seq_map),
        ],
        scratch_shapes=[
            pltpu.VMEM((N_SUB, N_LANE), F32),
            pltpu.VMEM((N_SUB, N_LANE), F32),
            pltpu.VMEM((N_LANE, w), F32),
            pltpu.VMEM((N_SUB, N_LANE), F32),
            pltpu.VMEM((N_LANE, w), F32),
        ],
    )
    return pl.pallas_call(
        body,
        grid_spec=grid_spec,
        out_shape=[
            jax.ShapeDtypeStruct((n_seq, n_new, N_SUB * w), BF16),
            jax.ShapeDtypeStruct((n_seq, n_new, 2 * N_SUB * HEAD_DIM), BF16),
        ],
        compiler_params=_cparams(("parallel", "arbitrary")),
        name="sample_attn",
    )(page_table, lam, q_a, q_b, nk_a, nv_a, nk_b, nv_b, g, *([ck_a, cv_a, ck_b, cv_b] * pps))


def _sample_operands(pbs, n_seq, n_new, width_a, width_b):
    qa, ka, va, qb, kb, vb = jnp.split(
        pbs, [width_a, 2 * width_a, 3 * width_a, 3 * width_a + width_b, 3 * width_a + 2 * width_b], axis=1)
    eye2 = jnp.eye(2, dtype=pbs.dtype)
    qa5 = qa.reshape(n_seq, n_new, N_SUB, 2, HEAD_DIM).transpose(0, 3, 4, 2, 1)
    q_a = (qa5[:, :, :, :, None, :] * eye2[None, :, None, None, :, None]).reshape(n_seq, 2 * HEAD_DIM, N_LANE)
    qb4 = qb.reshape(n_seq, n_new, 2 * N_SUB, HEAD_DIM).transpose(0, 3, 2, 1)
    par = (jnp.arange(2 * N_SUB)[None, :] % 2 == jnp.arange(2)[:, None]).astype(pbs.dtype)
    q_b = (qb4[:, None] * par[None, :, None, :, None]).reshape(n_seq, 2 * HEAD_DIM, N_LANE)
    nk_a = ka.reshape(n_seq, n_new * N_SUB, 2 * HEAD_DIM)
    nv_a = va.reshape(n_seq, n_new * N_SUB, 2 * HEAD_DIM)
    nk_b = kb.reshape(n_seq, n_new * N_SUB, 2 * HEAD_DIM)
    nv_b = vb.reshape(n_seq, n_new * N_SUB, 2 * HEAD_DIM)
    return q_a, q_b, nk_a, nv_a, nk_b, nv_b


def _outproj_body(oap_ref, obp_ref, xp_ref, oas_ref, obs_ref, xs_ref, wa_ref, wb_ref, o_ref, *, n_p):
    i = pl.program_id(0)

    def run(oa_ref, ob_ref, x_ref):
        acc = jnp.dot(oa_ref[...], wa_ref[...], preferred_element_type=F32)
        acc = acc + jnp.dot(ob_ref[...], wb_ref[...], preferred_element_type=F32)
        o_ref[...] = x_ref[...] + acc

    @pl.when(i < n_p)
    def _():
        run(oap_ref, obp_ref, xp_ref)

    @pl.when(i >= n_p)
    def _():
        run(oas_ref, obs_ref, xs_ref)


def _outproj(oa_p, ob_p, x_p, oa_s, ob_s, x_s, w_bf, *, tm, tn):
    rp, wa = oa_p.shape
    wb = ob_p.shape[1]
    d = w_bf.shape[1]
    assert rp % tm == 0 and oa_s.shape[0] == tm and wa == wb and d % tn == 0
    n_p = rp // tm

    def pmap(i, j):
        return (jnp.minimum(i, n_p - 1), 0)

    def pmap_x(i, j):
        return (jnp.minimum(i, n_p - 1), j)

    return pl.pallas_call(
        functools.partial(_outproj_body, n_p=n_p),
        grid=(n_p + 1, d // tn),
        in_specs=[
            pl.BlockSpec((tm, wa), pmap),
            pl.BlockSpec((tm, wb), pmap),
            pl.BlockSpec((tm, tn), pmap_x),
            pl.BlockSpec((tm, wa), lambda i, j: (0, 0)),
            pl.BlockSpec((tm, wb), lambda i, j: (0, 0)),
            pl.BlockSpec((tm, tn), lambda i, j: (0, j)),
            pl.BlockSpec((wa, tn), lambda i, j: (0, j)),
            pl.BlockSpec((wb, tn), lambda i, j: (1, j)),
        ],
        out_specs=pl.BlockSpec((tm, tn), lambda i, j: (i, j)),
        out_shape=jax.ShapeDtypeStruct((rp + tm, d), F32),
        compiler_params=_cparams(("parallel", "arbitrary")),
        name="outproj",
    )(oa_p, ob_p, x_p, oa_s, ob_s, x_s, w_bf, w_bf)


def _rmsnorm_f32(x, g):
    ms = jnp.mean(x * x, axis=-1, keepdims=True)
    return x * lax.rsqrt(ms + NORM_EPS) * g


def _first_lane(cond, lane_f):
    return jnp.min(jnp.where(cond, lane_f, float(N_LANE)), axis=1, keepdims=True)


def _router_body(h_ref, g_ref, whi_ref, wlo_ref, ids_ref, wts_ref):
    n = _rmsnorm_f32(h_ref[...], g_ref[...])
    hi, lo = _split_bf16(n)
    whi = whi_ref[...]
    logits = (jnp.dot(hi, whi, preferred_element_type=F32) + jnp.dot(lo, whi, preferred_element_type=F32)
              + jnp.dot(hi, wlo_ref[...], preferred_element_type=F32))
    lane = lax.broadcasted_iota(jnp.int32, logits.shape, 1)
    lane_f = lane.astype(F32)
    is_g = lane < N_GROUPS
    gl = jnp.where(is_g, logits, NEG_INF)
    gmax = jnp.max(gl, axis=1, keepdims=True)
    gsum = jnp.sum(jnp.exp(gl - gmax), axis=1, keepdims=True)
    g_gate = 1.0 / gsum
    g_idx = _first_lane(jnp.logical_and(is_g, gl == gmax), lane_f)
    e_lo = N_GROUPS + EXPERTS_PER_GROUP * g_idx
    sel = jnp.logical_and(lane_f >= e_lo, lane_f < e_lo + EXPERTS_PER_GROUP)
    el = jnp.where(sel, logits, NEG_INF)
    emax = jnp.max(el, axis=1, keepdims=True)
    ee = jnp.exp(el - emax)
    pe = ee / jnp.sum(ee, axis=1, keepdims=True)
    p1 = jnp.max(jnp.where(sel, pe, -1.0), axis=1, keepdims=True)
    i1 = _first_lane(jnp.logical_and(sel, pe == p1), lane_f)
    rest = jnp.logical_and(sel, lane_f != i1)
    p2 = jnp.max(jnp.where(rest, pe, -1.0), axis=1, keepdims=True)
    i2 = _first_lane(jnp.logical_and(rest, pe == p2), lane_f)
    denom = p1 + p2
    w1 = g_gate * p1 / denom
    w2 = g_gate * p2 / denom
    ids_ref[...] = jnp.where(lane == 0, i1 - N_GROUPS, jnp.where(lane == 1, i2 - N_GROUPS, 0.0)).astype(jnp.int32)
    wts_ref[...] = jnp.where(lane == 0, w1, jnp.where(lane == 1, w2, 0.0))


def _router(h, n_rows, g, w_hi, w_lo, *, tm):
    d = h.shape[1]
    assert n_rows % tm == 0
    return pl.pallas_call(
        _router_body,
        grid=(n_rows // tm,),
        in_specs=[
            pl.BlockSpec((tm, d), lambda i: (i, 0)),
            pl.BlockSpec((1, d), lambda i: (0, 0)),
            pl.BlockSpec((d, N_LANE), lambda i: (0, 0)),
            pl.BlockSpec((d, N_LANE), lambda i: (0, 0)),
        ],
        out_specs=[
            pl.BlockSpec((tm, N_LANE), lambda i: (i, 0)),
            pl.BlockSpec((tm, N_LANE), lambda i: (i, 0)),
        ],
        out_shape=[
            jax.ShapeDtypeStruct((n_rows, N_LANE), jnp.int32),
            jax.ShapeDtypeStruct((n_rows, N_LANE), F32),
        ],
        compiler_params=_cparams(("parallel",)),
        name="router",
    )(h, g, w_hi, w_lo)


def _dispatch_plan(ids, *, tm, n_split):
    t = ids.shape[0]
    e_flat = ids.reshape(-1)
    n_pairs = e_flat.shape[0]
    nt = (n_pairs + N_EXPERTS * (tm - 1)) // tm
    onehot = (e_flat[:, None] == jnp.arange(N_EXPERTS)[None, :]).astype(jnp.int32)
    counts = jnp.sum(onehot, axis=0)
    rank = jnp.sum((jnp.cumsum(onehot, axis=0) - 1) * onehot, axis=1)
    ntile = (counts + tm - 1) // tm
    tile_end = jnp.cumsum(ntile)
    tile_start = tile_end - ntile
    n_active = tile_end[-1]
    dest = jnp.sum(onehot * tile_start[None, :], axis=1) * tm + rank
    order = jnp.argsort(e_flat, stable=True)
    sorted_tokens = (order // 2).astype(jnp.int32)
    pair_start = jnp.cumsum(counts) - counts

    def count_le(bounds, x):
        return jnp.sum((bounds[None, :] <= x[:, None]).astype(jnp.int32), axis=1)

    tiles = jnp.arange(nt)
    tile_expert = jnp.minimum(count_le(tile_end, tiles), N_EXPERTS - 1)
    tile_rank0 = (tiles - tile_start[tile_expert]) * tm
    tile_src = pair_start[tile_expert] + tile_rank0
    tile_valid = jnp.where(tiles < n_active, jnp.clip(counts[tile_expert] - tile_rank0, 0, tm), 0)
    tile_tab = jnp.stack([tile_src, tile_valid], axis=0).astype(jnp.int32)
    n_items = nt * n_split
    n_act_items = n_active * n_split
    item_end = tile_end * n_split
    idx = jnp.arange(n_items)
    live = idx < n_act_items
    w = jnp.minimum(idx, n_act_items - 1)
    w_exp = jnp.minimum(count_le(item_end, w), N_EXPERTS - 1)
    local = w - tile_start[w_exp] * n_split
    w_nt = jnp.maximum(ntile[w_exp], 1)
    w_split = local // w_nt
    idle = idx - n_act_items
    o_tile = jnp.where(live, tile_start[w_exp] + local % w_nt, n_active + idle // n_split)
    o_split = jnp.where(live, w_split, idle % n_split)
    w_first = jnp.logical_and(local % w_nt == 0, live)
    items = jnp.stack([w_exp, w_split, o_tile, w_first.astype(jnp.int32), o_split], axis=0).astype(jnp.int32)
    counts_i = jnp.stack([n_active, n_active * n_split]).astype(jnp.int32)
    return dest.astype(jnp.int32), sorted_tokens, tile_tab, items, counts_i


ROW_CHUNK = 64


def _gather_norm_body(cnt_ref, tab_ref, stok_ref, g_ref, h_hbm, o_ref, buf, sem, *, tm):
    i = pl.program_id(0)
    n_act = cnt_ref[0]

    def row_copy(t, r, slot):
        src = stok_ref[tab_ref[0, t] + r]
        return pltpu.make_async_copy(h_hbm.at[pl.ds(src, 1)], buf.at[slot, pl.ds(r, 1)], sem.at[slot])

    def issue(t, slot):
        def start(r, c):
            row_copy(t, r, slot).start()
            return c

        lax.fori_loop(0, tab_ref[1, t], start, 0)

    @pl.when(i == 0)
    def _():
        buf[...] = jnp.zeros(buf.shape, F32)
        issue(0, 0)

    @pl.when(i + 1 < n_act)
    def _():
        issue(i + 1, (i + 1) % 2)

    @pl.when(i < n_act)
    def _():
        slot = i % 2
        n_real = tab_ref[1, i]

        @pl.when(n_real == tm)
        def _():
            pltpu.make_async_copy(h_hbm.at[pl.ds(0, tm)], buf.at[slot], sem.at[slot]).wait()

        @pl.when(n_real < tm)
        def _():
            def wait(r, c):
                row_copy(i, r, slot).wait()
                return c

            lax.fori_loop(0, n_real, wait, 0)

        g = g_ref[...]

        def chunk(c, carry):
            r0 = pl.multiple_of(c * ROW_CHUNK, ROW_CHUNK)
            o_ref[pl.ds(r0, ROW_CHUNK), :] = _rmsnorm_f32(buf[slot, pl.ds(r0, ROW_CHUNK), :], g).astype(o_ref.dtype)
            return carry

        lax.fori_loop(0, tm // ROW_CHUNK, chunk, 0)

    @pl.when(i >= n_act)
    def _():
        o_ref[...] = jnp.zeros(o_ref.shape, o_ref.dtype)


def _gather_norm(counts_i, tile_tab, sorted_tokens, g, h, *, tm):
    nt = tile_tab.shape[1]
    d = h.shape[1]
    grid_spec = pltpu.PrefetchScalarGridSpec(
        num_scalar_prefetch=3,
        grid=(nt,),
        in_specs=[
            pl.BlockSpec((1, d), lambda i, *_: (0, 0)),
            pl.BlockSpec(memory_space=pl.ANY),
        ],
        out_specs=pl.BlockSpec((tm, d), lambda i, *_: (i, 0)),
        scratch_shapes=[pltpu.VMEM((2, tm, d), F32), pltpu.SemaphoreType.DMA((2,))],
    )
    return pl.pallas_call(
        functools.partial(_gather_norm_body, tm=tm),
        grid_spec=grid_spec,
        out_shape=jax.ShapeDtypeStruct((nt * tm, d), BF16),
        compiler_params=_cparams(("arbitrary",)),
        name="gather_norm",
    )(counts_i, tile_tab, sorted_tokens, g, h)


def _expert_up_body(items_ref, cnt_ref, x_ref, wg_ref, wu_ref, o_ref, wg_s, wu_s):
    w = pl.program_id(0)

    @pl.when(items_ref[3, w] == 1)
    def _():
        wg_s[...] = wg_ref[0].astype(BF16)
        wu_s[...] = wu_ref[0].astype(BF16)

    @pl.when(w < cnt_ref[1])
    def _():
        x = x_ref[...]
        a = jnp.dot(x, wg_s[...], preferred_element_type=F32)
        u = jnp.dot(x, wu_s[...], preferred_element_type=F32)
        o_ref[...] = (a * (1.0 / (1.0 + jnp.exp(-a))) * u).astype(o_ref.dtype)

    @pl.when(w >= cnt_ref[1])
    def _():
        o_ref[...] = jnp.zeros(o_ref.shape, o_ref.dtype)


def _expert_up(items, counts_i, xs, w_gate, w_up, *, tm, tf):
    n_items = items.shape[1]
    d, ff = w_gate.shape[1:]
    grid_spec = pltpu.PrefetchScalarGridSpec(
        num_scalar_prefetch=2,
        grid=(n_items,),
        in_specs=[
            pl.BlockSpec((tm, d), lambda w, it, c: (it[2, w], 0)),
            pl.BlockSpec((1, d, tf), lambda w, it, c: (it[0, w], 0, it[1, w])),
            pl.BlockSpec((1, d, tf), lambda w, it, c: (it[0, w], 0, it[1, w])),
        ],
        out_specs=pl.BlockSpec((tm, tf), lambda w, it, c: (it[2, w], it[4, w])),
        scratch_shapes=[pltpu.VMEM((d, tf), BF16), pltpu.VMEM((d, tf), BF16)],
    )
    return pl.pallas_call(
        _expert_up_body,
        grid_spec=grid_spec,
        out_shape=jax.ShapeDtypeStruct((xs.shape[0], ff), BF16),
        compiler_params=_cparams(("arbitrary",)),
        name="expert_up",
    )(items, counts_i, xs, w_gate, w_up)


def _expert_down_body(items_ref, cnt_ref, a_ref, wd_ref, o_ref, wd_s):
    w = pl.program_id(0)

    @pl.when(items_ref[3, w] == 1)
    def _():
        wd_s[...] = wd_ref[0].astype(BF16)

    @pl.when(w < cnt_ref[1])
    def _():
        o_ref[...] = jnp.dot(a_ref[...], wd_s[...], preferred_element_type=F32)

    @pl.when(w >= cnt_ref[1])
    def _():
        o_ref[...] = jnp.zeros(o_ref.shape, o_ref.dtype)


def _expert_down(items, counts_i, act, w_down, *, tm, tn):
    n_items = items.shape[1]
    ff, d = w_down.shape[1:]
    grid_spec = pltpu.PrefetchScalarGridSpec(
        num_scalar_prefetch=2,
        grid=(n_items,),
        in_specs=[
            pl.BlockSpec((tm, ff), lambda w, it, c: (it[2, w], 0)),
            pl.BlockSpec((1, ff, tn), lambda w, it, c: (it[0, w], 0, it[1, w])),
        ],
        out_specs=pl.BlockSpec((tm, tn), lambda w, it, c: (it[2, w], it[4, w])),
        scratch_shapes=[pltpu.VMEM((ff, tn), BF16)],
    )
    return pl.pallas_call(
        _expert_down_body,
        grid_spec=grid_spec,
        out_shape=jax.ShapeDtypeStruct((act.shape[0], d), F32),
        compiler_params=_cparams(("arbitrary",)),
        name="expert_down",
    )(items, counts_i, act, w_down)


def _combine_body(dest_ref, h_ref, wts_ref, g_ref, y_hbm, op_ref, os_ref, buf, sem, *, tm, n_p):
    i = pl.program_id(0)

    def row_copy(t, r, k, slot):
        src = dest_ref[(t * tm + r) * 2 + k]
        return pltpu.make_async_copy(y_hbm.at[pl.ds(src, 1)], buf.at[slot, k, pl.ds(r, 1)], sem.at[slot])

    def issue(t, slot):
        def start(r, c):
            row_copy(t, r, 0, slot).start()
            row_copy(t, r, 1, slot).start()
            return c

        lax.fori_loop(0, tm, start, 0)

    @pl.when(i == 0)
    def _():
        issue(0, 0)

    @pl.when(i + 1 < pl.num_programs(0))
    def _():
        issue(i + 1, (i + 1) % 2)

    slot = i % 2
    for k in range(2):
        pltpu.make_async_copy(y_hbm.at[pl.ds(0, tm)], buf.at[slot, k], sem.at[slot]).wait()
    g = g_ref[...]

    def rows(o_ref):
        def chunk(c, carry):
            r0 = pl.multiple_of(c * ROW_CHUNK, ROW_CHUNK)
            sl = pl.ds(r0, ROW_CHUNK)
            wts = wts_ref[sl, :]
            out = h_ref[sl, :] + (wts[:, 0:1] * buf[slot, 0, sl, :] + wts[:, 1:2] * buf[slot, 1, sl, :])
            o_ref[sl, :] = _rmsnorm_f32(out, g)
            return carry

        lax.fori_loop(0, tm // ROW_CHUNK, chunk, 0)

    @pl.when(i < n_p)
    def _():
        rows(op_ref)

    @pl.when(i >= n_p)
    def _():
        rows(os_ref)


def _combine(dest, h, wts, g, y, *, tm, n_p, n_s):
    d = h.shape[1]
    grid_spec = pltpu.PrefetchScalarGridSpec(
        num_scalar_prefetch=1,
        grid=(n_p + n_s,),
        in_specs=[
            pl.BlockSpec((tm, d), lambda i, *_: (i, 0)),
            pl.BlockSpec((tm, N_LANE), lambda i, *_: (i, 0)),
            pl.BlockSpec((1, d), lambda i, *_: (0, 0)),
            pl.BlockSpec(memory_space=pl.ANY),
        ],
        out_specs=[
            pl.BlockSpec((tm, d), lambda i, *_: (jnp.minimum(i, n_p - 1), 0)),
            pl.BlockSpec((tm, d), lambda i, *_: (jnp.maximum(i - n_p, 0), 0)),
        ],
        scratch_shapes=[pltpu.VMEM((2, 2, tm, d), F32), pltpu.SemaphoreType.DMA((2,))],
    )
    return pl.pallas_call(
        functools.partial(_combine_body, tm=tm, n_p=n_p),
        grid_spec=grid_spec,
        out_shape=[
            jax.ShapeDtypeStruct((n_p * tm, d), F32),
            jax.ShapeDtypeStruct((n_s * tm, d), F32),
        ],
        compiler_params=_cparams(("arbitrary",)),
        name="combine",
    )(dest, h, wts, g, y)


def kernel(x_prompt, x_sample, cache_a_k, cache_a_v, cache_b_k, cache_b_v, page_table, meta_tokens, attn_norm, w_in, lam_q1, lam_k1, lam_q2, lam_k2, subln_norm, w_out, ffn_norm, w_router_group, w_router_expert, w_gate, w_up, w_down, final_norm):
    assert w_in.shape[0] == 1, "single-layer step"
    layer = 0
    b, seq, d = x_prompt.shape
    n_seq, n_new, _ = x_sample.shape
    n_pool, page = cache_a_k.shape[1], cache_a_k.shape[2]
    n_heads_a, n_heads_b = cache_a_v.shape[3], cache_b_k.shape[3]
    width_a, width_b = n_heads_a * 2 * HEAD_DIM, n_heads_b * HEAD_DIM
    past_len = page_table.shape[1] * page
    rows_p, rows_s = b * seq, n_seq * n_new
    wblk = 2 * HEAD_DIM

    lam_init = 0.8 - 0.6 * math.exp(-0.3 * layer)
    lam = (jnp.exp(jnp.sum(lam_q1[layer].astype(F32) * lam_k1[layer].astype(F32)))
           - jnp.exp(jnp.sum(lam_q2[layer].astype(F32) * lam_k2[layer].astype(F32))) + lam_init).reshape(1, 1)
    g_attn = attn_norm[layer].reshape(1, d)
    g_subln = subln_norm[layer].reshape(1, wblk)
    g_ffn = ffn_norm[layer].reshape(1, d)

    w_in_bf = w_in[layer].astype(BF16)
    xp = x_prompt.reshape(rows_p, d)
    cos_p, sin_p = _rope_tables(jnp.tile(N_META + jnp.arange(seq), b))
    pb, kv = _inproj(xp, g_attn, w_in_bf, cos_p, sin_p, tm=512, tn=1024, width_a=width_a, width_b=width_b)
    x_small = jnp.concatenate([x_sample.reshape(rows_s, d), meta_tokens.astype(F32)], axis=0)
    pos_small = jnp.concatenate([past_len + jnp.tile(jnp.arange(n_new), n_seq), jnp.arange(N_META)])
    cos_s, sin_s = _rope_tables(pos_small)
    pbs, kvs = _inproj(x_small, g_attn, w_in_bf, cos_s, sin_s, tm=rows_s + N_META, tn=512,
                       width_a=width_a, width_b=width_b)
    pb_meta = pbs[rows_s:]

    ca, cb = width_a // wblk, width_b // wblk
    pbt_meta = pb_meta.T
    oa_p = _attn_a(lam, pb, pb_meta, pbt_meta, g_subln, batch=b, seq=seq, n_heads=n_heads_a,
                    col_q=0, col_k=ca, col_v=2 * ca, lam_init=lam_init)
    ob_p = _attn_b(pb, pb_meta, pbt_meta, batch=b, seq=seq, n_heads=n_heads_b,
                    col_q=3 * ca, col_k=3 * ca + cb, col_v=3 * ca + 2 * cb)
    ops = _sample_operands(pbs[:rows_s], n_seq, n_new, width_a, width_b)
    oa_s, ob_s = _sample_attn(
        page_table, lam, *ops, g_subln,
        cache_a_k[layer].reshape(n_pool, page, 2 * n_heads_a, HEAD_DIM),
        cache_a_v[layer].reshape(n_pool, page, n_heads_a, wblk),
        cache_b_k[layer].reshape(n_pool, page, n_heads_b, HEAD_DIM),
        cache_b_v[layer].reshape(n_pool, page, n_heads_b, HEAD_DIM), lam_init=lam_init)

    tm_o = 512
    pad_s = lambda a: jnp.pad(a, ((0, tm_o - rows_s), (0, 0)))
    h = _outproj(oa_p, ob_p, xp, pad_s(oa_s.reshape(rows_s, width_a)), pad_s(ob_s.reshape(rows_s, width_b)),
                 pad_s(x_sample.reshape(rows_s, d)), w_out[layer].astype(BF16), tm=tm_o, tn=1024)

    tm_e = 256
    n_tok = rows_p + rows_s
    w_r = jnp.concatenate([w_router_group[layer], w_router_expert[layer].reshape(d, N_EXPERTS)], axis=1).astype(F32)
    w_r = jnp.pad(w_r, ((0, 0), (0, N_LANE - w_r.shape[1])))
    w_r_hi, w_r_lo = _split_bf16(w_r)
    ids, wts = _router(h, n_tok, g_ffn, w_r_hi, w_r_lo, tm=tm_e)
    n_split = 2
    dest, sorted_tokens, tile_tab, items, counts_i = _dispatch_plan(ids[:, :2], tm=tm_e, n_split=n_split)
    xs_sorted = _gather_norm(counts_i, tile_tab, sorted_tokens, g_ffn, h, tm=tm_e)
    act = _expert_up(items, counts_i, xs_sorted, w_gate[layer], w_up[layer], tm=tm_e, tf=w_gate.shape[3] // n_split)
    y = _expert_down(items, counts_i, act, w_down[layer], tm=tm_e, tn=d // n_split)
    y_p, y_s = _combine(dest, h, wts, final_norm.reshape(1, d), y,
                        tm=tm_e, n_p=rows_p // tm_e, n_s=rows_s // tm_e)

    def prompt_cache(lo, width, shape):
        meta = jnp.broadcast_to(kvs[rows_s:, lo:lo + width][None], (b, N_META, width))
        full = jnp.concatenate([meta, kv[:, lo:lo + width].reshape(b, seq, width)], axis=1)
        return full.reshape((1, b, seq + N_META) + shape)

    def sample_cache(lo, width, shape):
        return kvs[:rows_s, lo:lo + width].reshape((1, n_seq, n_new) + shape)

    shapes = [(0, width_a, (n_heads_a, 2, HEAD_DIM)), (width_a, width_a, (n_heads_a, wblk)),
              (2 * width_a, width_b, (n_heads_b, HEAD_DIM)), (2 * width_a + width_b, width_b, (n_heads_b, HEAD_DIM))]
    return ((y_p.reshape(b, seq, d), y_s.reshape(n_seq, n_new, d))
            + tuple(prompt_cache(*s) for s in shapes) + tuple(sample_cache(*s) for s in shapes))
```

```python
import functools
import math

import jax
import jax.numpy as jnp
from jax import lax
from jax.experimental import pallas as pl
from jax.experimental.pallas import tpu as pltpu

F32 = jnp.float32
BF16 = jnp.bfloat16

HEAD_DIM = 128
N_META = 16
ROPE_THETA = 10000.0
NORM_EPS = 1e-6
SUBLN_EPS = 1e-5
NEG_INF = -1e30
N_GROUPS = 4
EXPERTS_PER_GROUP = 8
N_EXPERTS = N_GROUPS * EXPERTS_PER_GROUP
LOG2E = math.log2(math.e)
QK_SCALE = HEAD_DIM ** -0.5 * LOG2E

VMEM_LIMIT_BYTES = 56 * 1024 * 1024


def _cparams(sem):
    return pltpu.CompilerParams(dimension_semantics=sem, vmem_limit_bytes=VMEM_LIMIT_BYTES)


def _inproj_norm(x_ref, g_ref, n_scr):
    x = x_ref[...]
    ms = jnp.mean(x * x, axis=-1, keepdims=True)
    n_scr[...] = (x * lax.rsqrt(ms + NORM_EPS) * g_ref[...]).astype(BF16)


def _inproj_tile(j, n, w_bf, cos_ref, sin_ref, *, tn, width_a, width_b):
    acc = jnp.dot(n, w_bf, preferred_element_type=F32)
    col = j * tn
    is_rope = col < 2 * width_a
    is_q = (col < width_a) | ((col >= 3 * width_a) & (col < 3 * width_a + width_b))
    cos = jnp.where(is_rope, cos_ref[...], 1.0)
    sin = jnp.where(is_rope, sin_ref[...], 0.0)
    chunks = []
    for c in range(tn // HEAD_DIM):
        a = acc[:, c * HEAD_DIM:(c + 1) * HEAD_DIM]
        chunks.append(a * cos + pltpu.roll(a, HEAD_DIM // 2, 1) * sin)
    val = jnp.concatenate(chunks, axis=1)
    scale = jnp.where(is_q, QK_SCALE, 1.0).astype(F32)
    return val, (val * scale).astype(BF16), is_q


def _inproj_small_body(x_ref, g_ref, w_ref, cos_ref, sin_ref, pb_ref, kv_ref, wbf_ref, n_scr, *, tn, width_a, width_b):
    j = pl.program_id(1)

    @pl.when(j == 0)
    def _():
        _inproj_norm(x_ref, g_ref, n_scr)

    w_bf = w_ref[...].astype(BF16)
    wbf_ref[...] = w_bf
    val, val_bf, is_q = _inproj_tile(j, n_scr[...], w_bf, cos_ref, sin_ref, tn=tn, width_a=width_a, width_b=width_b)
    pb_ref[...] = val_bf

    @pl.when(jnp.logical_not(is_q))
    def _():
        kv_ref[...] = val


def _inproj_small(x, g, w, cos2, sin2, *, tn, width_a, width_b):
    r, d = x.shape
    d_in = w.shape[1]
    assert d_in % tn == 0 and width_a % tn == 0 and width_b % tn == 0
    na, nb = width_a // tn, width_b // tn

    def kv_map(i, j):
        ja = jnp.clip(j - na, 0, 2 * na - 1)
        jb = jnp.clip(j - (3 * na + nb), 0, 2 * nb - 1) + 2 * na
        return (i, jnp.where(j < 3 * na, ja, jb))

    body = functools.partial(_inproj_small_body, tn=tn, width_a=width_a, width_b=width_b)
    return pl.pallas_call(
        body,
        grid=(1, d_in // tn),
        in_specs=[
            pl.BlockSpec((r, d), lambda i, j: (0, 0)),
            pl.BlockSpec((1, d), lambda i, j: (0, 0)),
            pl.BlockSpec((d, tn), lambda i, j: (0, j)),
            pl.BlockSpec((r, HEAD_DIM), lambda i, j: (0, 0)),
            pl.BlockSpec((r, HEAD_DIM), lambda i, j: (0, 0)),
        ],
        out_specs=[
            pl.BlockSpec((r, tn), lambda i, j: (0, j)),
            pl.BlockSpec((r, tn), kv_map),
            pl.BlockSpec((d, tn), lambda i, j: (0, j)),
        ],
        out_shape=[
            jax.ShapeDtypeStruct((r, d_in), BF16),
            jax.ShapeDtypeStruct((r, 2 * width_a + 2 * width_b), F32),
            jax.ShapeDtypeStruct((d, d_in), BF16),
        ],
        scratch_shapes=[pltpu.VMEM((r, d), BF16)],
        compiler_params=_cparams(("arbitrary", "arbitrary")),
        name="inproj_small",
    )(x, g, w, cos2, sin2)


N_CHUNK = 16


def _inproj_prompt_body(x_ref, g_ref, w_ref, cos_ref, sin_ref, meta_ref, pb_ref, ka_hbm, va_hbm, kb_hbm, vb_hbm,
                        n_scr, stage, sem, meta_sem, *, tm, tn, width_a, width_b, seq, n_meta):
    i, j = pl.program_id(0), pl.program_id(1)

    @pl.when(j == 0)
    def _():
        _inproj_norm(x_ref, g_ref, n_scr)

    @pl.when(jnp.logical_and(j == 0, i % (seq // tm) == 0))
    def _():
        first = (i // (seq // tm)) * (seq + n_meta)
        copies = [pltpu.make_async_copy(meta_ref.at[s], out_hbm.at[pl.ds(first, n_meta)], meta_sem)
                  for s, out_hbm in enumerate((ka_hbm, va_hbm, kb_hbm, vb_hbm))]
        for cp in copies:
            cp.start()
        for cp in copies:
            cp.wait()

    val, val_bf, _ = _inproj_tile(j, n_scr[...], w_ref[...], cos_ref, sin_ref, tn=tn, width_a=width_a, width_b=width_b)
    pb_ref[...] = val_bf

    na, nb = width_a // tn, width_b // tn
    segments = [(na, na, ka_hbm), (2 * na, na, va_hbm), (3 * na + nb, nb, kb_hbm), (3 * na + 2 * nb, nb, vb_hbm)]
    kv_tiles = [(j0 + t, t, out_hbm) for j0, n_tiles, out_hbm in segments for t in range(n_tiles)]
    tiles_per_batch = seq // tm
    row0 = (i // tiles_per_batch) * (seq + n_meta) + n_meta + (i % tiles_per_batch) * tm
    per_tile = tn // HEAD_DIM

    def chunk_copies(slot, t, out_hbm):
        return [pltpu.make_async_copy(stage.at[slot, :, c * HEAD_DIM:(c + 1) * HEAD_DIM],
                                      out_hbm.at[pl.ds(row0, tm), t * per_tile + c, :], sem.at[slot])
                for c in range(per_tile)]

    for n, (jt, t, out_hbm) in enumerate(kv_tiles):
        slot = n % 2

        @pl.when(j == jt)
        def _(n=n, t=t, slot=slot, out_hbm=out_hbm):
            @pl.when(jnp.logical_or(i > 0, n >= 2))
            def _():
                for cp in chunk_copies(slot, t, out_hbm):
                    cp.wait()

            stage[slot] = val
            for cp in chunk_copies(slot, t, out_hbm):
                cp.start()

    @pl.when(jnp.logical_and(i == pl.num_programs(0) - 1, j == pl.num_programs(1) - 1))
    def _():
        for slot in range(2):
            for cp in chunk_copies(slot, 0, vb_hbm):
                cp.wait()


def _inproj_prompt(x, g, w_bf, cos2, sin2, meta_kv, *, tm, tn, width_a, width_b, batch, seq, n_meta):
    r, d = x.shape
    assert meta_kv.shape == (4, n_meta, N_CHUNK, HEAD_DIM)
    d_in = w_bf.shape[1]
    assert r == batch * seq and seq % tm == 0 and d_in % tn == 0 and width_a % tn == 0 and width_b % tn == 0
    assert width_a == width_b == N_CHUNK * HEAD_DIM
    cache = jax.ShapeDtypeStruct((batch * (n_meta + seq), N_CHUNK, HEAD_DIM), F32)
    body = functools.partial(_inproj_prompt_body, tm=tm, tn=tn, width_a=width_a, width_b=width_b, seq=seq, n_meta=n_meta)
    return pl.pallas_call(
        body,
        grid=(r // tm, d_in // tn),
        in_specs=[
            pl.BlockSpec((tm, d), lambda i, j: (i, 0), pipeline_mode=pl.Buffered(1)),
            pl.BlockSpec((1, d), lambda i, j: (0, 0)),
            pl.BlockSpec((d, tn), lambda i, j: (0, j)),
            pl.BlockSpec((tm, HEAD_DIM), lambda i, j: (i, 0)),
            pl.BlockSpec((tm, HEAD_DIM), lambda i, j: (i, 0)),
            pl.BlockSpec((4, n_meta, N_CHUNK, HEAD_DIM), lambda i, j: (0, 0, 0, 0)),
        ],
        out_specs=[pl.BlockSpec((tm, tn), lambda i, j: (i, j))] + [pl.BlockSpec(memory_space=pl.ANY)] * 4,
        out_shape=[jax.ShapeDtypeStruct((r, d_in), BF16)] + [cache] * 4,
        scratch_shapes=[
            pltpu.VMEM((tm, d), BF16),
            pltpu.VMEM((2, tm, tn), F32),
            pltpu.SemaphoreType.DMA((2,)),
            pltpu.SemaphoreType.DMA(()),
        ],
        compiler_params=_cparams(("arbitrary", "arbitrary")),
        name="inproj",
    )(x, g, w_bf, cos2, sin2, meta_kv)


def _rope_tables(pos):
    half = HEAD_DIM // 2
    inv = ROPE_THETA ** (-jnp.arange(half, dtype=F32) / half)
    ang = pos.astype(F32)[:, None] * inv[None, :]
    cos, sin = jnp.cos(ang), jnp.sin(ang)
    return jnp.concatenate([cos, cos], axis=1), jnp.concatenate([-sin, sin], axis=1)


def _subln(d, g, lam_init):
    ms = jnp.mean(d * d, axis=-1, keepdims=True)
    return d * lax.rsqrt(ms + SUBLN_EPS) * g * (1.0 - lam_init)


def _split_bf16(x):
    hi = x.astype(BF16)
    lo = (x - hi.astype(F32)).astype(BF16)
    return hi, lo


KB = 256
QG = 256
NT_DIMS = (((1,), (1,)), ((), ()))


def _load_vt(v_ref, vt_scr):
    for c in range(vt_scr.shape[0]):
        vt_scr[c] = jnp.transpose(v_ref[c * KB:(c + 1) * KB, :].astype(F32)).astype(BF16)


def _attn_a_body(lam_ref, q_ref, k_ref, v_ref, mk_ref, mvt_ref, g_ref, o_ref, vt_scr, m_scr, l_scr, acc_scr,
                  *, tq, lam_init):
    qi = pl.program_id(2)

    @pl.when(qi == 0)
    def _():
        _load_vt(v_ref, vt_scr)

    n_hf = tq // QG
    chains = [(c, hf) for c in range(2) for hf in range(n_hf)]

    def scores(c, hf, k_blk):
        return lax.dot_general(k_blk[:, c * HEAD_DIM:(c + 1) * HEAD_DIM],
                               q_ref[hf * QG:(hf + 1) * QG, c * HEAD_DIM:(c + 1) * HEAD_DIM],
                               NT_DIMS, preferred_element_type=F32)

    off_d = pl.multiple_of(qi * KB, KB)
    kr = lax.broadcasted_iota(jnp.int32, (KB, QG), 0)
    qc = lax.broadcasted_iota(jnp.int32, (KB, QG), 1)
    for c, hf in chains:
        i = c * n_hf + hf
        s_d = jnp.where(kr <= qc + hf * QG, scores(c, hf, k_ref[pl.ds(off_d, KB), :]), NEG_INF)
        s_m = scores(c, hf, mk_ref[...])
        m = jnp.maximum(jnp.max(s_d, axis=0, keepdims=True), jnp.max(s_m, axis=0, keepdims=True))
        p_d = jnp.exp2(s_d - m)
        p_m = jnp.exp2(s_m - m)
        l_scr[i] = jnp.sum(p_d, axis=0, keepdims=True) + jnp.sum(p_m, axis=0, keepdims=True)
        acc_scr[i] = (jnp.dot(vt_scr[qi], p_d.astype(BF16), preferred_element_type=F32)
                      + jnp.dot(mvt_ref[...], p_m.astype(BF16), preferred_element_type=F32))
        m_scr[i] = m

    def full_blocks(first, n):
        off = pl.multiple_of(first * KB, KB)
        k_blk = k_ref[pl.ds(off, n * KB), :]
        ss = [scores(c, hf, k_blk) for c, hf in chains]
        alphas, ps = [], []
        for i, s in enumerate(ss):
            m_prev = m_scr[i]
            m_new = jnp.maximum(m_prev, jnp.max(s, axis=0, keepdims=True))
            alpha = jnp.exp2(m_prev - m_new)
            p32 = jnp.exp2(s - m_new)
            l_scr[i] = alpha * l_scr[i] + jnp.sum(p32, axis=0, keepdims=True)
            m_scr[i] = m_new
            alphas.append(alpha)
            ps.append(p32.astype(BF16))
        pvs = []
        for p in ps:
            pv = jnp.dot(vt_scr[first], p[:KB], preferred_element_type=F32)
            for t in range(1, n):
                pv = pv + jnp.dot(vt_scr[first + t], p[t * KB:(t + 1) * KB], preferred_element_type=F32)
            pvs.append(pv)
        for i in range(len(chains)):
            acc_scr[i] = alphas[i] * acc_scr[i] + pvs[i]

    @pl.when(qi % 2 == 1)
    def _():
        full_blocks(qi - 1, 1)

    def pair(t, carry):
        full_blocks(2 * t, 2)
        return carry

    lax.fori_loop(0, qi // 2, pair, 0)

    lam = lam_ref[0, 0]
    g = g_ref[...]
    for hf in range(n_hf):
        d_t = acc_scr[hf] / l_scr[hf] - lam * (acc_scr[n_hf + hf] / l_scr[n_hf + hf])
        o_ref[hf * QG:(hf + 1) * QG, :] = _subln(jnp.transpose(d_t), g, lam_init).astype(o_ref.dtype)


def _attn_a(lam, pb, pb_meta, pbt_meta, g, *, batch, seq, n_heads, col_q, col_k, col_v, lam_init):
    w = 2 * HEAD_DIM
    tq = KB
    nq = seq // tq
    return pl.pallas_call(
        functools.partial(_attn_a_body, tq=tq, lam_init=lam_init),
        grid=(batch, n_heads, nq),
        in_specs=[
            pl.BlockSpec(memory_space=pltpu.SMEM),
            pl.BlockSpec((tq, w), lambda b, h, i: (b * nq + i, col_q + h)),
            pl.BlockSpec((seq, w), lambda b, h, i: (b, col_k + h)),
            pl.BlockSpec((seq, w), lambda b, h, i: (b, col_v + h)),
            pl.BlockSpec((N_META, w), lambda b, h, i: (0, col_k + h)),
            pl.BlockSpec((w, N_META), lambda b, h, i: (col_v + h, 0)),
            pl.BlockSpec((1, w), lambda b, h, i: (0, 0)),
        ],
        out_specs=pl.BlockSpec((tq, w), lambda b, h, i: (b * nq + i, h)),
        out_shape=jax.ShapeDtypeStruct((batch * seq, n_heads * w), BF16),
        scratch_shapes=[
            pltpu.VMEM((seq // KB, w, KB), BF16),
            pltpu.VMEM((2 * tq // QG, 1, QG), F32),
            pltpu.VMEM((2 * tq // QG, 1, QG), F32),
            pltpu.VMEM((2 * tq // QG, w, QG), F32),
        ],
        compiler_params=_cparams(("parallel", "parallel", "arbitrary")),
        name="attn_a",
    )(lam, pb, pb, pb, pb_meta, pbt_meta, g)


def _neg_softplus2(z):
    return -jnp.maximum(z, 0.0) - jnp.log2(1.0 + jnp.exp2(-jnp.abs(z)))


def _attn_b_body(q_ref, k_ref, v_ref, mk_ref, mvt_ref, tri_ref, trim_ref, o_ref, vt_scr, carry_scr, acc_scr, *, tq):
    qi = pl.program_id(2)

    @pl.when(qi == 0)
    def _():
        _load_vt(v_ref, vt_scr)

    n_hf = tq // QG
    chains = [(hh, hf) for hh in range(2) for hf in range(n_hf)]

    n_ch = len(chains)

    def block(k_blk, vt_blk, tri, masks, first):
        zs, lms, parts = [], [], []
        for hh, hf in chains:
            zs.append(lax.dot_general(k_blk[:, hh * HEAD_DIM:(hh + 1) * HEAD_DIM],
                                      q_ref[hf * QG:(hf + 1) * QG, hh * HEAD_DIM:(hh + 1) * HEAD_DIM],
                                      NT_DIMS, preferred_element_type=F32))
        for i in range(n_ch):
            lm = _neg_softplus2(zs[i])
            if masks is not None:
                lm = jnp.where(masks[i], lm, 0.0)
            lms.append(lm)
            parts.extend(_split_bf16(lm))
        both = jnp.dot(tri, jnp.concatenate(parts, axis=1), preferred_element_type=F32)
        pvs = []
        for i, (hh, hf) in enumerate(chains):
            later = both[:, 2 * i * QG:(2 * i + 1) * QG] + both[:, (2 * i + 1) * QG:(2 * i + 2) * QG]
            e = lms[i] + zs[i] + later
            if not first:
                e = e + carry_scr[i]
            a = jnp.exp2(e)
            if masks is not None:
                a = jnp.where(masks[i], a, 0.0)
            pvs.append(jnp.dot(vt_blk[hh * HEAD_DIM:(hh + 1) * HEAD_DIM], a.astype(BF16),
                               preferred_element_type=F32))
        for i in range(n_ch):
            tot = jnp.sum(lms[i], axis=0, keepdims=True)
            if first:
                acc_scr[i] = pvs[i]
                carry_scr[i] = tot
            else:
                acc_scr[i] += pvs[i]
                carry_scr[i] += tot

    off_d = pl.multiple_of(qi * KB, KB)
    kr = lax.broadcasted_iota(jnp.int32, (KB, QG), 0)
    qc = lax.broadcasted_iota(jnp.int32, (KB, QG), 1)
    block(k_ref[pl.ds(off_d, KB), :], vt_scr[qi], tri_ref[...], [kr < qc + hf * QG for _, hf in chains], True)

    def body(t, carry):
        kb = qi - 1 - t
        off = pl.multiple_of(kb * KB, KB)
        block(k_ref[pl.ds(off, KB), :], vt_scr[kb], tri_ref[...], None, False)
        return carry

    lax.fori_loop(0, qi, body, 0)
    block(mk_ref[...], mvt_ref[...], trim_ref[...], None, False)
    for hf in range(n_hf):
        o_t = jnp.concatenate([acc_scr[hf], acc_scr[n_hf + hf]], axis=0)
        o_ref[hf * QG:(hf + 1) * QG, :] = jnp.transpose(o_t).astype(o_ref.dtype)


def _suffix_matrix_t(n):
    s = lax.broadcasted_iota(jnp.int32, (n, n), 0)
    j = lax.broadcasted_iota(jnp.int32, (n, n), 1)
    return (j > s).astype(BF16)


def _attn_b(pb, pb_meta, pbt_meta, *, batch, seq, n_heads, col_q, col_k, col_v):
    w = 2 * HEAD_DIM
    tq = KB
    nq = seq // tq
    return pl.pallas_call(
        functools.partial(_attn_b_body, tq=tq),
        grid=(batch, n_heads // 2, nq),
        in_specs=[
            pl.BlockSpec((tq, w), lambda b, h, i: (b * nq + i, col_q + h)),
            pl.BlockSpec((seq, w), lambda b, h, i: (b, col_k + h)),
            pl.BlockSpec((seq, w), lambda b, h, i: (b, col_v + h)),
            pl.BlockSpec((N_META, w), lambda b, h, i: (0, col_k + h)),
            pl.BlockSpec((w, N_META), lambda b, h, i: (col_v + h, 0)),
            pl.BlockSpec((KB, KB), lambda b, h, i: (0, 0)),
            pl.BlockSpec((N_META, N_META), lambda b, h, i: (0, 0)),
        ],
        out_specs=pl.BlockSpec((tq, w), lambda b, h, i: (b * nq + i, h)),
        out_shape=jax.ShapeDtypeStruct((batch * seq, n_heads * HEAD_DIM), BF16),
        scratch_shapes=[
            pltpu.VMEM((seq // KB, w, KB), BF16),
            pltpu.VMEM((2 * tq // QG, 1, QG), F32),
            pltpu.VMEM((2 * tq // QG, HEAD_DIM, QG), F32),
        ],
        compiler_params=_cparams(("parallel", "parallel", "arbitrary")),
        name="attn_b",
    )(pb, pb, pb, pb_meta, pbt_meta, _suffix_matrix_t(KB), _suffix_matrix_t(N_META))


SAMPLE_PAGES_PER_STEP = 4
N_SUB = 8
N_LANE = 128


def _diag_mask():
    sub = lax.broadcasted_iota(jnp.int32, (N_SUB, N_LANE), 0)
    lane = lax.broadcasted_iota(jnp.int32, (N_SUB, N_LANE), 1)
    return (lane // (N_LANE // N_SUB)) == sub


def _diag_to_cols(x, diag):
    row = jnp.sum(jnp.where(diag, x, 0.0), axis=0, keepdims=True)
    return jnp.transpose(jnp.broadcast_to(row, (N_LANE, N_LANE)))


def _suffix_excl(x):
    n = x.shape[0]
    zero = jnp.zeros(x.shape[1:], F32)
    if n <= 8:
        outs, run = [None] * n, zero
        for j in range(n - 1, -1, -1):
            outs[j] = run
            run = run + x[j]
        return jnp.stack(outs, axis=0), run
    ng = n // 8
    g = x.reshape(ng, 8, N_SUB, N_LANE)
    inner, run = [None] * 8, jnp.zeros((ng, N_SUB, N_LANE), F32)
    for j in range(7, -1, -1):
        inner[j] = run
        run = run + g[:, j]
    offs, tot = [None] * ng, zero
    for i in range(ng - 1, -1, -1):
        offs[i] = tot
        tot = tot + run[i]
    off = jnp.stack(offs, axis=0)
    later = jnp.stack([w + off for w in inner], axis=1)
    return later.reshape(n, N_SUB, N_LANE), tot


def _sample_block(k_a, v_a, k_b, v_b, q_a, q_b, ntok, mask_a, mask_b, diag, state):
    m_a, l_a, acc_a, carry_b, acc_b = state
    rows = ntok * N_SUB
    tn_dims = (((0,), (0,)), ((), ()))
    s = jnp.dot(k_a, q_a, preferred_element_type=F32).reshape(ntok, N_SUB, N_LANE)
    z = jnp.dot(k_b, q_b, preferred_element_type=F32).reshape(ntok, N_SUB, N_LANE)
    if mask_a is not None:
        s = jnp.where(mask_a, s, NEG_INF)
    m_old = m_a[...]
    m_new = jnp.maximum(m_old, jnp.max(s, axis=0))
    alpha = jnp.exp2(m_old - m_new)
    p = jnp.where(diag, jnp.exp2(s - m_new), 0.0)
    l_a[...] = alpha * l_a[...] + jnp.sum(p, axis=0)
    m_a[...] = m_new
    pv_a = lax.dot_general(p.reshape(rows, N_LANE).astype(BF16), v_a, tn_dims, preferred_element_type=F32)
    lm = _neg_softplus2(z)
    ls = lm + z
    keep = diag
    if mask_b is not None:
        lm = jnp.where(mask_b, lm, 0.0)
        keep = jnp.logical_and(mask_b, diag)
    later, total = _suffix_excl(lm)
    a = jnp.where(keep, jnp.exp2(ls + later + carry_b[...]), 0.0).reshape(rows, N_LANE)
    pv_b = lax.dot_general(a.astype(BF16), v_b, tn_dims, preferred_element_type=F32)
    a_col = _diag_to_cols(alpha, diag)
    acc_a[...] = acc_a[...] * jnp.concatenate([a_col, a_col], axis=1) + pv_a
    acc_b[...] += pv_b
    carry_b[...] += total


def _sample_attn_body(pt_ref, lam_ref, qa_ref, qb_ref, nka_ref, nva_ref, nkb_ref, nvb_ref, g_ref, *rest,
                      lam_init, page, n_new, pps):
    page_refs, (oa_ref, ob_ref, m_a, l_a, acc_a, carry_b, acc_b) = rest[:4 * pps], rest[4 * pps:]
    p = pl.program_id(1)
    diag = _diag_mask()
    state = (m_a, l_a, acc_a, carry_b, acc_b)
    q_a, q_b = qa_ref[0], qb_ref[0]

    @pl.when(p == 0)
    def _():
        m_a[...] = jnp.full(m_a.shape, NEG_INF, F32)
        l_a[...] = jnp.zeros(l_a.shape, F32)
        acc_a[...] = jnp.zeros(acc_a.shape, F32)
        carry_b[...] = jnp.zeros(carry_b.shape, F32)
        acc_b[...] = jnp.zeros(acc_b.shape, F32)
        tok = lax.broadcasted_iota(jnp.int32, (n_new, N_SUB, N_LANE), 0)
        qpos = lax.broadcasted_iota(jnp.int32, (n_new, N_SUB, N_LANE), 2) % 8
        _sample_block(nka_ref[0], nva_ref[0], nkb_ref[0], nvb_ref[0], q_a, q_b, n_new,
                      tok <= qpos, tok < qpos, diag, state)

    @pl.when(p > 0)
    def _():
        rows = page * N_SUB

        def halves(ref):
            h0 = ref[0, :, pl.ds(0, N_SUB, stride=2), :].reshape(rows, HEAD_DIM).astype(BF16)
            h1 = ref[0, :, pl.ds(1, N_SUB, stride=2), :].reshape(rows, HEAD_DIM).astype(BF16)
            return h0, h1

        for j in range(pps):
            cak_ref, cav_ref, cbk_ref, cbv_ref = page_refs[4 * j:4 * j + 4]
            k_a = jnp.concatenate(halves(cak_ref), axis=1)
            v_a = cav_ref[0].reshape(rows, 2 * HEAD_DIM).astype(BF16)
            k_b = jnp.concatenate(halves(cbk_ref), axis=1)
            v_b = jnp.concatenate(halves(cbv_ref), axis=1)
            _sample_block(k_a, v_a, k_b, v_b, q_a, q_b, page, None, None, diag, state)

    @pl.when(p == pl.num_programs(1) - 1)
    def _():
        l_col = _diag_to_cols(l_a[...], diag)
        o = acc_a[...] / jnp.concatenate([l_col, l_col], axis=1)
        lam = lam_ref[0, 0]
        g = g_ref[...]
        w = 2 * HEAD_DIM
        for h in range(N_SUB):
            d = o[h * 16:h * 16 + 8] - lam * o[h * 16 + 8:h * 16 + 16]
            oa_ref[0, :, h * w:(h + 1) * w] = _subln(d, g, lam_init).astype(oa_ref.dtype)
        ob = acc_b[...]
        for h in range(2 * N_SUB):
            par = (h % 2) * HEAD_DIM
            ob_ref[0, :, h * HEAD_DIM:(h + 1) * HEAD_DIM] = ob[h * 8:(h + 1) * 8, par:par + HEAD_DIM].astype(ob_ref.dtype)


def _sample_attn(page_table, lam, q_a, q_b, nk_a, nv_a, nk_b, nv_b, g, ck_a, cv_a, ck_b, cv_b, *, lam_init):
    n_seq, n_pages = page_table.shape
    page = ck_a.shape[1]
    n_new = nk_a.shape[1] // N_SUB
    assert q_a.shape[1:] == (2 * HEAD_DIM, N_LANE) and cv_a.shape[2:] == (N_SUB, 2 * HEAD_DIM)
    assert ck_a.shape[2:] == (2 * N_SUB, HEAD_DIM) and ck_b.shape[2:] == (2 * N_SUB, HEAD_DIM)

    def seq_map(s, p, pt):
        return (s, 0, 0)

    pps = SAMPLE_PAGES_PER_STEP
    assert n_pages % pps == 0

    def page_map(j):
        return lambda s, p, pt: (pt[s, n_pages - 1 - (jnp.maximum(p - 1, 0) * pps + j)], 0, 0, 0)

    w = 2 * HEAD_DIM
    page_specs = []
    for j in range(pps):
        page_specs += [
            pl.BlockSpec((1, page, 2 * N_SUB, HEAD_DIM), page_map(j)),
            pl.BlockSpec((1, page, N_SUB, w), page_map(j)),
            pl.BlockSpec((1, page, 2 * N_SUB, HEAD_DIM), page_map(j)),
            pl.BlockSpec((1, page, 2 * N_SUB, HEAD_DIM), page_map(j)),
        ]
    body = functools.partial(_sample_attn_body, lam_init=lam_init, page=page, n_new=n_new, pps=pps)
    grid_spec = pltpu.PrefetchScalarGridSpec(
        num_scalar_prefetch=1,
        grid=(n_seq, n_pages // pps + 1),
        in_specs=[
            pl.BlockSpec(memory_space=pltpu.SMEM),
            pl.BlockSpec((1, w, N_LANE), seq_map),
            pl.BlockSpec((1, w, N_LANE), seq_map),
            pl.BlockSpec((1, n_new * N_SUB, w), seq_map),
            pl.BlockSpec((1, n_new * N_SUB, w), seq_map),
            pl.BlockSpec((1, n_new * N_SUB, w), seq_map),
            pl.BlockSpec((1, n_new * N_SUB, w), seq_map),
            pl.BlockSpec((1, w), lambda s, p, pt: (0, 0)),
        ] + page_specs,
        out_specs=[
            pl.BlockSpec((1, n_new, N_SUB * w), seq_map),
            pl.BlockSpec((1, n_new, 2 * N_SUB * HEAD_DIM), seq_map),
        ],
        scratch_shapes=[
            pltpu.VMEM((N_SUB, N_LANE), F32),
            pltpu.VMEM((N_SUB, N_LANE), F32),
            pltpu.VMEM((N_LANE, w), F32),
            pltpu.VMEM((N_SUB, N_LANE), F32),
            pltpu.VMEM((N_LANE, w), F32),
        ],
    )
    return pl.pallas_call(
        body,
        grid_spec=grid_spec,
        out_shape=[
            jax.ShapeDtypeStruct((n_seq, n_new, N_SUB * w), BF16),
            jax.ShapeDtypeStruct((n_seq, n_new, 2 * N_SUB * HEAD_DIM), BF16),
        ],
        compiler_params=_cparams(("parallel", "arbitrary")),
        name="sample_attn",
    )(page_table, lam, q_a, q_b, nk_a, nv_a, nk_b, nv_b, g, *([ck_a, cv_a, ck_b, cv_b] * pps))


def _sample_operands(pbs, n_seq, n_new, width_a, width_b):
    qa, ka, va, qb, kb, vb = jnp.split(
        pbs, [width_a, 2 * width_a, 3 * width_a, 3 * width_a + width_b, 3 * width_a + 2 * width_b], axis=1)
    eye2 = jnp.eye(2, dtype=pbs.dtype)
    qa5 = qa.reshape(n_seq, n_new, N_SUB, 2, HEAD_DIM).transpose(0, 3, 4, 2, 1)
    q_a = (qa5[:, :, :, :, None, :] * eye2[None, :, None, None, :, None]).reshape(n_seq, 2 * HEAD_DIM, N_LANE)
    qb4 = qb.reshape(n_seq, n_new, 2 * N_SUB, HEAD_DIM).transpose(0, 3, 2, 1)
    par = (jnp.arange(2 * N_SUB)[None, :] % 2 == jnp.arange(2)[:, None]).astype(pbs.dtype)
    q_b = (qb4[:, None] * par[None, :, None, :, None]).reshape(n_seq, 2 * HEAD_DIM, N_LANE)
    nk_a = ka.reshape(n_seq, n_new * N_SUB, 2 * HEAD_DIM)
    nv_a = va.reshape(n_seq, n_new * N_SUB, 2 * HEAD_DIM)
    nk_b = kb.reshape(n_seq, n_new * N_SUB, 2 * HEAD_DIM)
    nv_b = vb.reshape(n_seq, n_new * N_SUB, 2 * HEAD_DIM)
    return q_a, q_b, nk_a, nv_a, nk_b, nv_b


def _outproj_body(oap_ref, obp_ref, xp_ref, oas_ref, obs_ref, xs_ref, wa_ref, wb_ref, o_ref, *, n_p):
    i = pl.program_id(0)

    def run(oa_ref, ob_ref, x_ref):
        acc = jnp.dot(oa_ref[...], wa_ref[...], preferred_element_type=F32)
        acc = acc + jnp.dot(ob_ref[...], wb_ref[...], preferred_element_type=F32)
        o_ref[...] = x_ref[...] + acc

    @pl.when(i < n_p)
    def _():
        run(oap_ref, obp_ref, xp_ref)

    @pl.when(i >= n_p)
    def _():
        run(oas_ref, obs_ref, xs_ref)


def _outproj(oa_p, ob_p, x_p, oa_s, ob_s, x_s, w_bf, *, tm, tn):
    rp, wa = oa_p.shape
    wb = ob_p.shape[1]
    d = w_bf.shape[1]
    assert rp % tm == 0 and oa_s.shape[0] == tm and wa == wb and d % tn == 0
    n_p = rp // tm

    def pmap(i, j):
        return (jnp.minimum(i, n_p - 1), 0)

    def pmap_x(i, j):
        return (jnp.minimum(i, n_p - 1), j)

    return pl.pallas_call(
        functools.partial(_outproj_body, n_p=n_p),
        grid=(n_p + 1, d // tn),
        in_specs=[
            pl.BlockSpec((tm, wa), pmap),
            pl.BlockSpec((tm, wb), pmap),
            pl.BlockSpec((tm, tn), pmap_x),
            pl.BlockSpec((tm, wa), lambda i, j: (0, 0)),
            pl.BlockSpec((tm, wb), lambda i, j: (0, 0)),
            pl.BlockSpec((tm, tn), lambda i, j: (0, j)),
            pl.BlockSpec((wa, tn), lambda i, j: (0, j)),
            pl.BlockSpec((wb, tn), lambda i, j: (1, j)),
        ],
        out_specs=pl.BlockSpec((tm, tn), lambda i, j: (i, j)),
        out_shape=jax.ShapeDtypeStruct((rp + tm, d), F32),
        compiler_params=_cparams(("parallel", "arbitrary")),
        name="outproj",
    )(oa_p, ob_p, x_p, oa_s, ob_s, x_s, w_bf, w_bf)


def _rmsnorm_f32(x, g):
    ms = jnp.mean(x * x, axis=-1, keepdims=True)
    return x * lax.rsqrt(ms + NORM_EPS) * g


def _first_lane(cond, lane_f):
    return jnp.min(jnp.where(cond, lane_f, float(N_LANE)), axis=1, keepdims=True)


def _router_body(h_ref, g_ref, whi_ref, wlo_ref, ids_ref, wts_ref):
    n = _rmsnorm_f32(h_ref[...], g_ref[...])
    hi, lo = _split_bf16(n)
    whi = whi_ref[...]
    logits = (jnp.dot(hi, whi, preferred_element_type=F32) + jnp.dot(lo, whi, preferred_element_type=F32)
              + jnp.dot(hi, wlo_ref[...], preferred_element_type=F32))
    lane = lax.broadcasted_iota(jnp.int32, logits.shape, 1)
    lane_f = lane.astype(F32)
    is_g = lane < N_GROUPS
    gl = jnp.where(is_g, logits, NEG_INF)
    gmax = jnp.max(gl, axis=1, keepdims=True)
    gsum = jnp.sum(jnp.exp(gl - gmax), axis=1, keepdims=True)
    g_gate = 1.0 / gsum
    g_idx = _first_lane(jnp.logical_and(is_g, gl == gmax), lane_f)
    e_lo = N_GROUPS + EXPERTS_PER_GROUP * g_idx
    sel = jnp.logical_and(lane_f >= e_lo, lane_f < e_lo + EXPERTS_PER_GROUP)
    el = jnp.where(sel, logits, NEG_INF)
    emax = jnp.max(el, axis=1, keepdims=True)
    ee = jnp.exp(el - emax)
    pe = ee / jnp.sum(ee, axis=1, keepdims=True)
    p1 = jnp.max(jnp.where(sel, pe, -1.0), axis=1, keepdims=True)
    i1 = _first_lane(jnp.logical_and(sel, pe == p1), lane_f)
    rest = jnp.logical_and(sel, lane_f != i1)
    p2 = jnp.max(jnp.where(rest, pe, -1.0), axis=1, keepdims=True)
    i2 = _first_lane(jnp.logical_and(rest, pe == p2), lane_f)
    denom = p1 + p2
    w1 = g_gate * p1 / denom
    w2 = g_gate * p2 / denom
    ids_ref[...] = jnp.where(lane == 0, i1 - N_GROUPS, jnp.where(lane == 1, i2 - N_GROUPS, 0.0)).astype(jnp.int32)
    wts_ref[...] = jnp.where(lane == 0, w1, jnp.where(lane == 1, w2, 0.0))


def _router(h, n_rows, g, w_hi, w_lo, *, tm):
    d = h.shape[1]
    assert n_rows % tm == 0
    return pl.pallas_call(
        _router_body,
        grid=(n_rows // tm,),
        in_specs=[
            pl.BlockSpec((tm, d), lambda i: (i, 0)),
            pl.BlockSpec((1, d), lambda i: (0, 0)),
            pl.BlockSpec((d, N_LANE), lambda i: (0, 0)),
            pl.BlockSpec((d, N_LANE), lambda i: (0, 0)),
        ],
        out_specs=[
            pl.BlockSpec((tm, N_LANE), lambda i: (i, 0)),
            pl.BlockSpec((tm, N_LANE), lambda i: (i, 0)),
        ],
        out_shape=[
            jax.ShapeDtypeStruct((n_rows, N_LANE), jnp.int32),
            jax.ShapeDtypeStruct((n_rows, N_LANE), F32),
        ],
        compiler_params=_cparams(("parallel",)),
        name="router",
    )(h, g, w_hi, w_lo)


def _dispatch_plan(ids, *, tm, n_split):
    t = ids.shape[0]
    e_flat = ids.reshape(-1)
    n_pairs = e_flat.shape[0]
    nt = (n_pairs + N_EXPERTS * (tm - 1)) // tm
    onehot = (e_flat[:, None] == jnp.arange(N_EXPERTS)[None, :]).astype(jnp.int32)
    counts = jnp.sum(onehot, axis=0)
    rank = jnp.sum((jnp.cumsum(onehot, axis=0) - 1) * onehot, axis=1)
    ntile = (counts + tm - 1) // tm
    tile_end = jnp.cumsum(ntile)
    tile_start = tile_end - ntile
    n_active = tile_end[-1]
    dest = jnp.sum(onehot * tile_start[None, :], axis=1) * tm + rank
    order = jnp.argsort(e_flat, stable=True)
    sorted_tokens = (order // 2).astype(jnp.int32)
    pair_start = jnp.cumsum(counts) - counts

    def count_le(bounds, x):
        return jnp.sum((bounds[None, :] <= x[:, None]).astype(jnp.int32), axis=1)

    tiles = jnp.arange(nt)
    tile_expert = jnp.minimum(count_le(tile_end, tiles), N_EXPERTS - 1)
    tile_rank0 = (tiles - tile_start[tile_expert]) * tm
    tile_src = pair_start[tile_expert] + tile_rank0
    tile_valid = jnp.where(tiles < n_active, jnp.clip(counts[tile_expert] - tile_rank0, 0, tm), 0)
    tile_tab = jnp.stack([tile_src, tile_valid], axis=0).astype(jnp.int32)
    n_items = nt * n_split
    n_act_items = n_active * n_split
    item_end = tile_end * n_split
    idx = jnp.arange(n_items)
    live = idx < n_act_items
    w = jnp.minimum(idx, n_act_items - 1)
    w_exp = jnp.minimum(count_le(item_end, w), N_EXPERTS - 1)
    local = w - tile_start[w_exp] * n_split
    w_nt = jnp.maximum(ntile[w_exp], 1)
    w_split = local // w_nt
    idle = idx - n_act_items
    o_tile = jnp.where(live, tile_start[w_exp] + local % w_nt, n_active + idle // n_split)
    o_split = jnp.where(live, w_split, idle % n_split)
    w_first = jnp.logical_and(local % w_nt == 0, live)
    items = jnp.stack([w_exp, w_split, o_tile, w_first.astype(jnp.int32), o_split], axis=0).astype(jnp.int32)
    counts_i = jnp.stack([n_active, n_active * n_split]).astype(jnp.int32)
    return dest.astype(jnp.int32), sorted_tokens, tile_tab, items, counts_i


ROW_CHUNK = 64


def _gather_norm_body(cnt_ref, tab_ref, stok_ref, g_ref, h_hbm, o_ref, buf, sem, *, tm):
    i = pl.program_id(0)
    n_act = cnt_ref[0]

    def row_copy(t, r, slot):
        src = stok_ref[tab_ref[0, t] + r]
        return pltpu.make_async_copy(h_hbm.at[pl.ds(src, 1)], buf.at[slot, pl.ds(r, 1)], sem.at[slot])

    def issue(t, slot):
        def start(r, c):
            row_copy(t, r, slot).start()
            return c

        lax.fori_loop(0, tab_ref[1, t], start, 0)

    @pl.when(i == 0)
    def _():
        buf[...] = jnp.zeros(buf.shape, F32)
        issue(0, 0)

    @pl.when(i + 1 < n_act)
    def _():
        issue(i + 1, (i + 1) % 2)

    @pl.when(i < n_act)
    def _():
        slot = i % 2
        n_real = tab_ref[1, i]

        @pl.when(n_real == tm)
        def _():
            pltpu.make_async_copy(h_hbm.at[pl.ds(0, tm)], buf.at[slot], sem.at[slot]).wait()

        @pl.when(n_real < tm)
        def _():
            def wait(r, c):
                row_copy(i, r, slot).wait()
                return c

            lax.fori_loop(0, n_real, wait, 0)

        g = g_ref[...]

        def chunk(c, carry):
            r0 = pl.multiple_of(c * ROW_CHUNK, ROW_CHUNK)
            o_ref[pl.ds(r0, ROW_CHUNK), :] = _rmsnorm_f32(buf[slot, pl.ds(r0, ROW_CHUNK), :], g).astype(o_ref.dtype)
            return carry

        lax.fori_loop(0, tm // ROW_CHUNK, chunk, 0)

    @pl.when(i >= n_act)
    def _():
        o_ref[...] = jnp.zeros(o_ref.shape, o_ref.dtype)


def _gather_norm(counts_i, tile_tab, sorted_tokens, g, h, *, tm):
    nt = tile_tab.shape[1]
    d = h.shape[1]
    grid_spec = pltpu.PrefetchScalarGridSpec(
        num_scalar_prefetch=3,
        grid=(nt,),
        in_specs=[
            pl.BlockSpec((1, d), lambda i, *_: (0, 0)),
            pl.BlockSpec(memory_space=pl.ANY),
        ],
        out_specs=pl.BlockSpec((tm, d), lambda i, *_: (i, 0)),
        scratch_shapes=[pltpu.VMEM((2, tm, d), F32), pltpu.SemaphoreType.DMA((2,))],
    )
    return pl.pallas_call(
        functools.partial(_gather_norm_body, tm=tm),
        grid_spec=grid_spec,
        out_shape=jax.ShapeDtypeStruct((nt * tm, d), BF16),
        compiler_params=_cparams(("arbitrary",)),
        name="gather_norm",
    )(counts_i, tile_tab, sorted_tokens, g, h)


def _expert_up_body(items_ref, cnt_ref, x_ref, wg_ref, wu_ref, o_ref, wg_s, wu_s):
    w = pl.program_id(0)

    @pl.when(items_ref[3, w] == 1)
    def _():
        wg_s[...] = wg_ref[0].astype(BF16)
        wu_s[...] = wu_ref[0].astype(BF16)

    @pl.when(w < cnt_ref[1])
    def _():
        x = x_ref[...]
        a = jnp.dot(x, wg_s[...], preferred_element_type=F32)
        u = jnp.dot(x, wu_s[...], preferred_element_type=F32)
        o_ref[...] = (a * (1.0 / (1.0 + jnp.exp(-a))) * u).astype(o_ref.dtype)

    @pl.when(w >= cnt_ref[1])
    def _():
        o_ref[...] = jnp.zeros(o_ref.shape, o_ref.dtype)


def _expert_up(items, counts_i, xs, w_gate, w_up, *, tm, tf):
    n_items = items.shape[1]
    d, ff = w_gate.shape[1:]
    grid_spec = pltpu.PrefetchScalarGridSpec(
        num_scalar_prefetch=2,
        grid=(n_items,),
        in_specs=[
            pl.BlockSpec((tm, d), lambda w, it, c: (it[2, w], 0)),
            pl.BlockSpec((1, d, tf), lambda w, it, c: (it[0, w], 0, it[1, w])),
            pl.BlockSpec((1, d, tf), lambda w, it, c: (it[0, w], 0, it[1, w])),
        ],
        out_specs=pl.BlockSpec((tm, tf), lambda w, it, c: (it[2, w], it[4, w])),
        scratch_shapes=[pltpu.VMEM((d, tf), BF16), pltpu.VMEM((d, tf), BF16)],
    )
    return pl.pallas_call(
        _expert_up_body,
        grid_spec=grid_spec,
        out_shape=jax.ShapeDtypeStruct((xs.shape[0], ff), BF16),
        compiler_params=_cparams(("arbitrary",)),
        name="expert_up",
    )(items, counts_i, xs, w_gate, w_up)


def _expert_down_body(items_ref, cnt_ref, a_ref, wd_ref, o_ref, wd_s):
    w = pl.program_id(0)

    @pl.when(items_ref[3, w] == 1)
    def _():
        wd_s[...] = wd_ref[0].astype(BF16)

    @pl.when(w < cnt_ref[1])
    def _():
        o_ref[...] = jnp.dot(a_ref[...], wd_s[...], preferred_element_type=F32)

    @pl.when(w >= cnt_ref[1])
    def _():
        o_ref[...] = jnp.zeros(o_ref.shape, o_ref.dtype)


def _expert_down(items, counts_i, act, w_down, *, tm, tn):
    n_items = items.shape[1]
    ff, d = w_down.shape[1:]
    grid_spec = pltpu.PrefetchScalarGridSpec(
        num_scalar_prefetch=2,
        grid=(n_items,),
        in_specs=[
            pl.BlockSpec((tm, ff), lambda w, it, c: (it[2, w], 0)),
            pl.BlockSpec((1, ff, tn), lambda w, it, c: (it[0, w], 0, it[1, w])),
        ],
        out_specs=pl.BlockSpec((tm, tn), lambda w, it, c: (it[2, w], it[4, w])),
        scratch_shapes=[pltpu.VMEM((ff, tn), BF16)],
    )
    return pl.pallas_call(
        _expert_down_body,
        grid_spec=grid_spec,
        out_shape=jax.ShapeDtypeStruct((act.shape[0], d), F32),
        compiler_params=_cparams(("arbitrary",)),
        name="expert_down",
    )(items, counts_i, act, w_down)


def _combine_body(dest_ref, h_ref, wts_ref, g_ref, y_hbm, op_ref, os_ref, buf, sem, *, tm, n_p):
    i = pl.program_id(0)

    def row_copy(t, r, k, slot):
        src = dest_ref[(t * tm + r) * 2 + k]
        return pltpu.make_async_copy(y_hbm.at[pl.ds(src, 1)], buf.at[slot, k, pl.ds(r, 1)], sem.at[slot])

    def issue(t, slot):
        def start(r, c):
            row_copy(t, r, 0, slot).start()
            row_copy(t, r, 1, slot).start()
            return c

        lax.fori_loop(0, tm, start, 0)

    @pl.when(i == 0)
    def _():
        issue(0, 0)

    @pl.when(i + 1 < pl.num_programs(0))
    def _():
        issue(i + 1, (i + 1) % 2)

    slot = i % 2
    for k in range(2):
        pltpu.make_async_copy(y_hbm.at[pl.ds(0, tm)], buf.at[slot, k], sem.at[slot]).wait()
    g = g_ref[...]

    def rows(o_ref):
        def chunk(c, carry):
            r0 = pl.multiple_of(c * ROW_CHUNK, ROW_CHUNK)
            sl = pl.ds(r0, ROW_CHUNK)
            wts = wts_ref[sl, :]
            out = h_ref[sl, :] + (wts[:, 0:1] * buf[slot, 0, sl, :] + wts[:, 1:2] * buf[slot, 1, sl, :])
            o_ref[sl, :] = _rmsnorm_f32(out, g)
            return carry

        lax.fori_loop(0, tm // ROW_CHUNK, chunk, 0)

    @pl.when(i < n_p)
    def _():
        rows(op_ref)

    @pl.when(i >= n_p)
    def _():
        rows(os_ref)


def _combine(dest, h, wts, g, y, *, tm, n_p, n_s):
    d = h.shape[1]
    grid_spec = pltpu.PrefetchScalarGridSpec(
        num_scalar_prefetch=1,
        grid=(n_p + n_s,),
        in_specs=[
            pl.BlockSpec((tm, d), lambda i, *_: (i, 0)),
            pl.BlockSpec((tm, N_LANE), lambda i, *_: (i, 0)),
            pl.BlockSpec((1, d), lambda i, *_: (0, 0)),
            pl.BlockSpec(memory_space=pl.ANY),
        ],
        out_specs=[
            pl.BlockSpec((tm, d), lambda i, *_: (jnp.minimum(i, n_p - 1), 0)),
            pl.BlockSpec((tm, d), lambda i, *_: (jnp.maximum(i - n_p, 0), 0)),
        ],
        scratch_shapes=[pltpu.VMEM((2, 2, tm, d), F32), pltpu.SemaphoreType.DMA((2,))],
    )
    return pl.pallas_call(
        functools.partial(_combine_body, tm=tm, n_p=n_p),
        grid_spec=grid_spec,
        out_shape=[
            jax.ShapeDtypeStruct((n_p * tm, d), F32),
            jax.ShapeDtypeStruct((n_s * tm, d), F32),
        ],
        compiler_params=_cparams(("arbitrary",)),
        name="combine",
    )(dest, h, wts, g, y)


def kernel(x_prompt, x_sample, cache_a_k, cache_a_v, cache_b_k, cache_b_v, page_table, meta_tokens, attn_norm, w_in, lam_q1, lam_k1, lam_q2, lam_k2, subln_norm, w_out, ffn_norm, w_router_group, w_router_expert, w_gate, w_up, w_down, final_norm):
    assert w_in.shape[0] == 1, "single-layer step"
    layer = 0
    b, seq, d = x_prompt.shape
    n_seq, n_new, _ = x_sample.shape
    n_pool, page = cache_a_k.shape[1], cache_a_k.shape[2]
    n_heads_a, n_heads_b = cache_a_v.shape[3], cache_b_k.shape[3]
    width_a, width_b = n_heads_a * 2 * HEAD_DIM, n_heads_b * HEAD_DIM
    past_len = page_table.shape[1] * page
    rows_p, rows_s = b * seq, n_seq * n_new
    wblk = 2 * HEAD_DIM

    lam_init = 0.8 - 0.6 * math.exp(-0.3 * layer)
    lam = (jnp.exp(jnp.sum(lam_q1[layer].astype(F32) * lam_k1[layer].astype(F32)))
           - jnp.exp(jnp.sum(lam_q2[layer].astype(F32) * lam_k2[layer].astype(F32))) + lam_init).reshape(1, 1)
    g_attn = attn_norm[layer].reshape(1, d)
    g_subln = subln_norm[layer].reshape(1, wblk)
    g_ffn = ffn_norm[layer].reshape(1, d)

    x_small = jnp.concatenate([x_sample.reshape(rows_s, d), meta_tokens.astype(F32)], axis=0)
    pos_small = jnp.concatenate([past_len + jnp.tile(jnp.arange(n_new), n_seq), jnp.arange(N_META)])
    cos_s, sin_s = _rope_tables(pos_small)
    pbs, kvs, w_in_bf = _inproj_small(x_small, g_attn, w_in[layer], cos_s, sin_s, tn=512,
                                      width_a=width_a, width_b=width_b)
    pb_meta = pbs[rows_s:]
    xp = x_prompt.reshape(rows_p, d)
    cos_p, sin_p = _rope_tables(jnp.tile(N_META + jnp.arange(seq), b))
    meta_kv = kvs[rows_s:].reshape(N_META, 4, N_CHUNK, HEAD_DIM).transpose(1, 0, 2, 3)
    pb, *prompt_caches = _inproj_prompt(xp, g_attn, w_in_bf, cos_p, sin_p, meta_kv, tm=512, tn=1024, width_a=width_a,
                                        width_b=width_b, batch=b, seq=seq, n_meta=N_META)

    ca, cb = width_a // wblk, width_b // wblk
    pbt_meta = pb_meta.T
    oa_p = _attn_a(lam, pb, pb_meta, pbt_meta, g_subln, batch=b, seq=seq, n_heads=n_heads_a,
                    col_q=0, col_k=ca, col_v=2 * ca, lam_init=lam_init)
    ob_p = _attn_b(pb, pb_meta, pbt_meta, batch=b, seq=seq, n_heads=n_heads_b,
                    col_q=3 * ca, col_k=3 * ca + cb, col_v=3 * ca + 2 * cb)
    ops = _sample_operands(pbs[:rows_s], n_seq, n_new, width_a, width_b)
    oa_s, ob_s = _sample_attn(
        page_table, lam, *ops, g_subln,
        cache_a_k[layer].reshape(n_pool, page, 2 * n_heads_a, HEAD_DIM),
        cache_a_v[layer].reshape(n_pool, page, n_heads_a, wblk),
        cache_b_k[layer].reshape(n_pool, page, n_heads_b, HEAD_DIM),
        cache_b_v[layer].reshape(n_pool, page, n_heads_b, HEAD_DIM), lam_init=lam_init)

    tm_o = 512
    pad_s = lambda a: jnp.pad(a, ((0, tm_o - rows_s), (0, 0)))
    h = _outproj(oa_p, ob_p, xp, pad_s(oa_s.reshape(rows_s, width_a)), pad_s(ob_s.reshape(rows_s, width_b)),
                 pad_s(x_sample.reshape(rows_s, d)), w_out[layer].astype(BF16), tm=tm_o, tn=1024)

    tm_e = 256
    n_tok = rows_p + rows_s
    w_r = jnp.concatenate([w_router_group[layer], w_router_expert[layer].reshape(d, N_EXPERTS)], axis=1).astype(F32)
    w_r = jnp.pad(w_r, ((0, 0), (0, N_LANE - w_r.shape[1])))
    w_r_hi, w_r_lo = _split_bf16(w_r)
    ids, wts = _router(h, n_tok, g_ffn, w_r_hi, w_r_lo, tm=tm_e)
    n_split = 2
    dest, sorted_tokens, tile_tab, items, counts_i = _dispatch_plan(ids[:, :2], tm=tm_e, n_split=n_split)
    xs_sorted = _gather_norm(counts_i, tile_tab, sorted_tokens, g_ffn, h, tm=tm_e)
    act = _expert_up(items, counts_i, xs_sorted, w_gate[layer], w_up[layer], tm=tm_e, tf=w_gate.shape[3] // n_split)
    y = _expert_down(items, counts_i, act, w_down[layer], tm=tm_e, tn=d // n_split)
    y_p, y_s = _combine(dest, h, wts, final_norm.reshape(1, d), y,
                        tm=tm_e, n_p=rows_p // tm_e, n_s=rows_s // tm_e)

    def prompt_cache(cache, lo, width, shape):
        return cache.reshape((1, b, seq + N_META) + shape)

    def sample_cache(lo, width, shape):
        return kvs[:rows_s, lo:lo + width].reshape((1, n_seq, n_new) + shape)

    shapes = [(0, width_a, (n_heads_a, 2, HEAD_DIM)), (width_a, width_a, (n_heads_a, wblk)),
              (2 * width_a, width_b, (n_heads_b, HEAD_DIM)), (2 * width_a + width_b, width_b, (n_heads_b, HEAD_DIM))]
    return ((y_p.reshape(b, seq, d), y_s.reshape(n_seq, n_new, d))
            + tuple(prompt_cache(c, *s) for c, s in zip(prompt_caches, shapes))
            + tuple(sample_cache(*s) for s in shapes))
```

```python
import functools
import math

import jax
import jax.numpy as jnp
from jax import lax
from jax.experimental import pallas as pl
from jax.experimental.pallas import tpu as pltpu

F32 = jnp.float32
BF16 = jnp.bfloat16

HEAD_DIM = 128
N_META = 16
ROPE_THETA = 10000.0
NORM_EPS = 1e-6
SUBLN_EPS = 1e-5
NEG_INF = -1e30
N_GROUPS = 4
EXPERTS_PER_GROUP = 8
N_EXPERTS = N_GROUPS * EXPERTS_PER_GROUP
LOG2E = math.log2(math.e)
QK_SCALE = HEAD_DIM ** -0.5 * LOG2E

VMEM_LIMIT_BYTES = 56 * 1024 * 1024


def _cparams(sem):
    return pltpu.CompilerParams(dimension_semantics=sem, vmem_limit_bytes=VMEM_LIMIT_BYTES)


def _inproj_norm(x_ref, g_ref, n_scr):
    x = x_ref[...]
    ms = jnp.mean(x * x, axis=-1, keepdims=True)
    n_scr[...] = (x * lax.rsqrt(ms + NORM_EPS) * g_ref[...]).astype(BF16)


def _inproj_tile(j, n, w_bf, cos_ref, sin_ref, *, tn, width_a, width_b):
    acc = jnp.dot(n, w_bf, preferred_element_type=F32)
    col = j * tn
    is_rope = col < 2 * width_a
    is_q = (col < width_a) | ((col >= 3 * width_a) & (col < 3 * width_a + width_b))
    cos = jnp.where(is_rope, cos_ref[...], 1.0)
    sin = jnp.where(is_rope, sin_ref[...], 0.0)
    chunks = []
    for c in range(tn // HEAD_DIM):
        a = acc[:, c * HEAD_DIM:(c + 1) * HEAD_DIM]
        chunks.append(a * cos + pltpu.roll(a, HEAD_DIM // 2, 1) * sin)
    val = jnp.concatenate(chunks, axis=1)
    scale = jnp.where(is_q, QK_SCALE, 1.0).astype(F32)
    return val, (val * scale).astype(BF16), is_q


def _inproj_small_body(x_ref, g_ref, w_ref, cos_ref, sin_ref, pb_ref, kv_ref, wbf_ref, n_scr, *, tn, width_a, width_b):
    j = pl.program_id(1)

    @pl.when(j == 0)
    def _():
        _inproj_norm(x_ref, g_ref, n_scr)

    w_bf = w_ref[...].astype(BF16)
    wbf_ref[...] = w_bf
    val, val_bf, is_q = _inproj_tile(j, n_scr[...], w_bf, cos_ref, sin_ref, tn=tn, width_a=width_a, width_b=width_b)
    pb_ref[...] = val_bf

    @pl.when(jnp.logical_not(is_q))
    def _():
        kv_ref[...] = val


def _inproj_small(x, g, w, cos2, sin2, *, tn, width_a, width_b):
    r, d = x.shape
    d_in = w.shape[1]
    assert d_in % tn == 0 and width_a % tn == 0 and width_b % tn == 0
    na, nb = width_a // tn, width_b // tn

    def kv_map(i, j):
        ja = jnp.clip(j - na, 0, 2 * na - 1)
        jb = jnp.clip(j - (3 * na + nb), 0, 2 * nb - 1) + 2 * na
        return (i, jnp.where(j < 3 * na, ja, jb))

    body = functools.partial(_inproj_small_body, tn=tn, width_a=width_a, width_b=width_b)
    return pl.pallas_call(
        body,
        grid=(1, d_in // tn),
        in_specs=[
            pl.BlockSpec((r, d), lambda i, j: (0, 0)),
            pl.BlockSpec((1, d), lambda i, j: (0, 0)),
            pl.BlockSpec((d, tn), lambda i, j: (0, j)),
            pl.BlockSpec((r, HEAD_DIM), lambda i, j: (0, 0)),
            pl.BlockSpec((r, HEAD_DIM), lambda i, j: (0, 0)),
        ],
        out_specs=[
            pl.BlockSpec((r, tn), lambda i, j: (0, j)),
            pl.BlockSpec((r, tn), kv_map),
            pl.BlockSpec((d, tn), lambda i, j: (0, j)),
        ],
        out_shape=[
            jax.ShapeDtypeStruct((r, d_in), BF16),
            jax.ShapeDtypeStruct((r, 2 * width_a + 2 * width_b), F32),
            jax.ShapeDtypeStruct((d, d_in), BF16),
        ],
        scratch_shapes=[pltpu.VMEM((r, d), BF16)],
        compiler_params=_cparams(("arbitrary", "arbitrary")),
        name="inproj_small",
    )(x, g, w, cos2, sin2)


N_CHUNK = 16


def _inproj_prompt_body(x_ref, g_ref, w_ref, cos_ref, sin_ref, meta_ref, pb_ref, ka_hbm, va_hbm, kb_hbm, vb_hbm,
                        n_scr, stage, sem, meta_sem, *, tm, tn, width_a, width_b, seq, n_meta):
    i, j = pl.program_id(0), pl.program_id(1)

    @pl.when(j == 0)
    def _():
        _inproj_norm(x_ref, g_ref, n_scr)

    @pl.when(jnp.logical_and(j == 0, i % (seq // tm) == 0))
    def _():
        first = (i // (seq // tm)) * (seq + n_meta)
        copies = [pltpu.make_async_copy(meta_ref.at[s], out_hbm.at[pl.ds(first, n_meta)], meta_sem)
                  for s, out_hbm in enumerate((ka_hbm, va_hbm, kb_hbm, vb_hbm))]
        for cp in copies:
            cp.start()
        for cp in copies:
            cp.wait()

    val, val_bf, _ = _inproj_tile(j, n_scr[...], w_ref[...], cos_ref, sin_ref, tn=tn, width_a=width_a, width_b=width_b)
    pb_ref[...] = val_bf

    na, nb = width_a // tn, width_b // tn
    segments = [(na, na, ka_hbm), (2 * na, na, va_hbm), (3 * na + nb, nb, kb_hbm), (3 * na + 2 * nb, nb, vb_hbm)]
    kv_tiles = [(j0 + t, t, out_hbm) for j0, n_tiles, out_hbm in segments for t in range(n_tiles)]
    tiles_per_batch = seq // tm
    row0 = (i // tiles_per_batch) * (seq + n_meta) + n_meta + (i % tiles_per_batch) * tm
    per_tile = tn // HEAD_DIM

    def chunk_copies(slot, t, out_hbm):
        return [pltpu.make_async_copy(stage.at[slot, :, c * HEAD_DIM:(c + 1) * HEAD_DIM],
                                      out_hbm.at[pl.ds(row0, tm), t * per_tile + c, :], sem.at[slot])
                for c in range(per_tile)]

    for n, (jt, t, out_hbm) in enumerate(kv_tiles):
        slot = n % 2

        @pl.when(j == jt)
        def _(n=n, t=t, slot=slot, out_hbm=out_hbm):
            @pl.when(jnp.logical_or(i > 0, n >= 2))
            def _():
                for cp in chunk_copies(slot, t, out_hbm):
                    cp.wait()

            stage[slot] = val
            for cp in chunk_copies(slot, t, out_hbm):
                cp.start()

    @pl.when(jnp.logical_and(i == pl.num_programs(0) - 1, j == pl.num_programs(1) - 1))
    def _():
        for slot in range(2):
            for cp in chunk_copies(slot, 0, vb_hbm):
                cp.wait()


def _inproj_prompt(x, g, w_bf, cos2, sin2, meta_kv, *, tm, tn, width_a, width_b, batch, seq, n_meta):
    r, d = x.shape
    assert meta_kv.shape == (4, n_meta, N_CHUNK, HEAD_DIM)
    d_in = w_bf.shape[1]
    assert r == batch * seq and seq % tm == 0 and d_in % tn == 0 and width_a % tn == 0 and width_b % tn == 0
    assert width_a == width_b == N_CHUNK * HEAD_DIM
    cache = jax.ShapeDtypeStruct((batch * (n_meta + seq), N_CHUNK, HEAD_DIM), F32)
    body = functools.partial(_inproj_prompt_body, tm=tm, tn=tn, width_a=width_a, width_b=width_b, seq=seq, n_meta=n_meta)
    return pl.pallas_call(
        body,
        grid=(r // tm, d_in // tn),
        in_specs=[
            pl.BlockSpec((tm, d), lambda i, j: (i, 0), pipeline_mode=pl.Buffered(1)),
            pl.BlockSpec((1, d), lambda i, j: (0, 0)),
            pl.BlockSpec((d, tn), lambda i, j: (0, j)),
            pl.BlockSpec((tm, HEAD_DIM), lambda i, j: (i, 0)),
            pl.BlockSpec((tm, HEAD_DIM), lambda i, j: (i, 0)),
            pl.BlockSpec((4, n_meta, N_CHUNK, HEAD_DIM), lambda i, j: (0, 0, 0, 0)),
        ],
        out_specs=[pl.BlockSpec((tm, tn), lambda i, j: (i, j))] + [pl.BlockSpec(memory_space=pl.ANY)] * 4,
        out_shape=[jax.ShapeDtypeStruct((r, d_in), BF16)] + [cache] * 4,
        scratch_shapes=[
            pltpu.VMEM((tm, d), BF16),
            pltpu.VMEM((2, tm, tn), F32),
            pltpu.SemaphoreType.DMA((2,)),
            pltpu.SemaphoreType.DMA(()),
        ],
        compiler_params=_cparams(("arbitrary", "arbitrary")),
        name="inproj",
    )(x, g, w_bf, cos2, sin2, meta_kv)


def _rope_tables(pos):
    half = HEAD_DIM // 2
    inv = ROPE_THETA ** (-jnp.arange(half, dtype=F32) / half)
    ang = pos.astype(F32)[:, None] * inv[None, :]
    cos, sin = jnp.cos(ang), jnp.sin(ang)
    return jnp.concatenate([cos, cos], axis=1), jnp.concatenate([-sin, sin], axis=1)


def _subln(d, g, lam_init):
    ms = jnp.mean(d * d, axis=-1, keepdims=True)
    return d * lax.rsqrt(ms + SUBLN_EPS) * g * (1.0 - lam_init)


def _split_bf16(x):
    hi = x.astype(BF16)
    lo = (x - hi.astype(F32)).astype(BF16)
    return hi, lo


KB = 256
QG = 256
NT_DIMS = (((1,), (1,)), ((), ()))


def _load_vt(v_ref, vt_scr):
    for c in range(vt_scr.shape[0]):
        vt_scr[c] = jnp.transpose(v_ref[c * KB:(c + 1) * KB, :].astype(F32)).astype(BF16)


def _attn_a_body(lam_ref, q_ref, k_ref, v_ref, mk_ref, mvt_ref, g_ref, o_ref, vt_scr, m_scr, l_scr, acc_scr,
                  *, tq, lam_init):
    qi = pl.program_id(2)

    @pl.when(qi == 0)
    def _():
        _load_vt(v_ref, vt_scr)

    n_hf = tq // QG
    chains = [(c, hf) for c in range(2) for hf in range(n_hf)]

    def scores(c, hf, k_blk):
        return lax.dot_general(k_blk[:, c * HEAD_DIM:(c + 1) * HEAD_DIM],
                               q_ref[hf * QG:(hf + 1) * QG, c * HEAD_DIM:(c + 1) * HEAD_DIM],
                               NT_DIMS, preferred_element_type=F32)

    off_d = pl.multiple_of(qi * KB, KB)
    kr = lax.broadcasted_iota(jnp.int32, (KB, QG), 0)
    qc = lax.broadcasted_iota(jnp.int32, (KB, QG), 1)
    for c, hf in chains:
        i = c * n_hf + hf
        s_d = jnp.where(kr <= qc + hf * QG, scores(c, hf, k_ref[pl.ds(off_d, KB), :]), NEG_INF)
        s_m = scores(c, hf, mk_ref[...])
        m = jnp.maximum(jnp.max(s_d, axis=0, keepdims=True), jnp.max(s_m, axis=0, keepdims=True))
        p_d = jnp.exp2(s_d - m)
        p_m = jnp.exp2(s_m - m)
        l_scr[i] = jnp.sum(p_d, axis=0, keepdims=True) + jnp.sum(p_m, axis=0, keepdims=True)
        acc_scr[i] = (jnp.dot(vt_scr[qi], p_d.astype(BF16), preferred_element_type=F32)
                      + jnp.dot(mvt_ref[...], p_m.astype(BF16), preferred_element_type=F32))
        m_scr[i] = m

    def full_blocks(first, n):
        off = pl.multiple_of(first * KB, KB)
        k_blk = k_ref[pl.ds(off, n * KB), :]
        ss = [scores(c, hf, k_blk) for c, hf in chains]
        alphas, ps = [], []
        for i, s in enumerate(ss):
            m_prev = m_scr[i]
            m_new = jnp.maximum(m_prev, jnp.max(s, axis=0, keepdims=True))
            alpha = jnp.exp2(m_prev - m_new)
            p32 = jnp.exp2(s - m_new)
            l_scr[i] = alpha * l_scr[i] + jnp.sum(p32, axis=0, keepdims=True)
            m_scr[i] = m_new
            alphas.append(alpha)
            ps.append(p32.astype(BF16))
        pvs = []
        for p in ps:
            pv = jnp.dot(vt_scr[first], p[:KB], preferred_element_type=F32)
            for t in range(1, n):
                pv = pv + jnp.dot(vt_scr[first + t], p[t * KB:(t + 1) * KB], preferred_element_type=F32)
            pvs.append(pv)
        for i in range(len(chains)):
            acc_scr[i] = alphas[i] * acc_scr[i] + pvs[i]

    @pl.when(qi % 2 == 1)
    def _():
        full_blocks(qi - 1, 1)

    def pair(t, carry):
        full_blocks(2 * t, 2)
        return carry

    lax.fori_loop(0, qi // 2, pair, 0)

    lam = lam_ref[0, 0]
    g = g_ref[...]
    for hf in range(n_hf):
        d_t = acc_scr[hf] / l_scr[hf] - lam * (acc_scr[n_hf + hf] / l_scr[n_hf + hf])
        o_ref[hf * QG:(hf + 1) * QG, :] = _subln(jnp.transpose(d_t), g, lam_init).astype(o_ref.dtype)


def _attn_a(lam, pb, pb_meta, pbt_meta, g, *, batch, seq, n_heads, col_q, col_k, col_v, lam_init):
    w = 2 * HEAD_DIM
    tq = KB
    nq = seq // tq
    return pl.pallas_call(
        functools.partial(_attn_a_body, tq=tq, lam_init=lam_init),
        grid=(batch, n_heads, nq),
        in_specs=[
            pl.BlockSpec(memory_space=pltpu.SMEM),
            pl.BlockSpec((tq, w), lambda b, h, i: (b * nq + i, col_q + h)),
            pl.BlockSpec((seq, w), lambda b, h, i: (b, col_k + h)),
            pl.BlockSpec((seq, w), lambda b, h, i: (b, col_v + h)),
            pl.BlockSpec((N_META, w), lambda b, h, i: (0, col_k + h)),
            pl.BlockSpec((w, N_META), lambda b, h, i: (col_v + h, 0)),
            pl.BlockSpec((1, w), lambda b, h, i: (0, 0)),
        ],
        out_specs=pl.BlockSpec((tq, w), lambda b, h, i: (b * nq + i, h)),
        out_shape=jax.ShapeDtypeStruct((batch * seq, n_heads * w), BF16),
        scratch_shapes=[
            pltpu.VMEM((seq // KB, w, KB), BF16),
            pltpu.VMEM((2 * tq // QG, 1, QG), F32),
            pltpu.VMEM((2 * tq // QG, 1, QG), F32),
            pltpu.VMEM((2 * tq // QG, w, QG), F32),
        ],
        compiler_params=_cparams(("parallel", "parallel", "arbitrary")),
        name="attn_a",
    )(lam, pb, pb, pb, pb_meta, pbt_meta, g)


def _neg_softplus2(z):
    return -jnp.maximum(z, 0.0) - jnp.log2(1.0 + jnp.exp2(-jnp.abs(z)))


def _attn_b_body(q_ref, k_ref, v_ref, mk_ref, mvt_ref, tri_ref, trim_ref, o_ref, vt_scr, carry_scr, acc_scr, *, tq):
    qi = pl.program_id(2)

    @pl.when(qi == 0)
    def _():
        _load_vt(v_ref, vt_scr)

    n_hf = tq // QG
    chains = [(hh, hf) for hh in range(2) for hf in range(n_hf)]

    n_ch = len(chains)

    def blocks(kv_blks, tri, masks, first):
        units = [(u, i) for u in range(len(kv_blks)) for i in range(n_ch)]
        zs, lms, parts, tots = [], [], [], []
        for u, i in units:
            hh, hf = chains[i]
            zs.append(lax.dot_general(kv_blks[u][0][:, hh * HEAD_DIM:(hh + 1) * HEAD_DIM],
                                      q_ref[hf * QG:(hf + 1) * QG, hh * HEAD_DIM:(hh + 1) * HEAD_DIM],
                                      NT_DIMS, preferred_element_type=F32))
        for n, (u, i) in enumerate(units):
            lm = _neg_softplus2(zs[n])
            if masks is not None:
                lm = jnp.where(masks[i], lm, 0.0)
            lms.append(lm)
            parts.extend(_split_bf16(lm))
            tots.append(jnp.sum(lm, axis=0, keepdims=True))
        both = jnp.dot(tri, jnp.concatenate(parts, axis=1), preferred_element_type=F32)
        pvs = [None] * n_ch
        for n, (u, i) in enumerate(units):
            hh, hf = chains[i]
            later = both[:, 2 * n * QG:(2 * n + 1) * QG] + both[:, (2 * n + 1) * QG:(2 * n + 2) * QG]
            e = lms[n] + zs[n] + later
            if not first:
                e = e + carry_scr[i]
            for u2 in range(u):
                e = e + tots[u2 * n_ch + i]
            a = jnp.exp2(e)
            if masks is not None:
                a = jnp.where(masks[i], a, 0.0)
            pv = jnp.dot(kv_blks[u][1][hh * HEAD_DIM:(hh + 1) * HEAD_DIM], a.astype(BF16),
                         preferred_element_type=F32)
            pvs[i] = pv if pvs[i] is None else pvs[i] + pv
        for i in range(n_ch):
            tot = tots[i]
            for u in range(1, len(kv_blks)):
                tot = tot + tots[u * n_ch + i]
            if first:
                acc_scr[i] = pvs[i]
                carry_scr[i] = tot
            else:
                acc_scr[i] += pvs[i]
                carry_scr[i] += tot

    def kv_block(kb):
        off = pl.multiple_of(kb * KB, KB)
        return k_ref[pl.ds(off, KB), :], vt_scr[kb]

    kr = lax.broadcasted_iota(jnp.int32, (KB, QG), 0)
    qc = lax.broadcasted_iota(jnp.int32, (KB, QG), 1)
    blocks([kv_block(qi)], tri_ref[...], [kr < qc + hf * QG for _, hf in chains], True)

    def pair(t, carry):
        kb = qi - 1 - 2 * t
        blocks([kv_block(kb), kv_block(kb - 1)], tri_ref[...], None, False)
        return carry

    lax.fori_loop(0, qi // 2, pair, 0)

    @pl.when(qi % 2 == 1)
    def _():
        blocks([kv_block(0)], tri_ref[...], None, False)

    blocks([(mk_ref[...], mvt_ref[...])], trim_ref[...], None, False)
    for hf in range(n_hf):
        o_t = jnp.concatenate([acc_scr[hf], acc_scr[n_hf + hf]], axis=0)
        o_ref[hf * QG:(hf + 1) * QG, :] = jnp.transpose(o_t).astype(o_ref.dtype)


def _suffix_matrix_t(n):
    s = lax.broadcasted_iota(jnp.int32, (n, n), 0)
    j = lax.broadcasted_iota(jnp.int32, (n, n), 1)
    return (j > s).astype(BF16)


def _attn_b(pb, pb_meta, pbt_meta, *, batch, seq, n_heads, col_q, col_k, col_v):
    w = 2 * HEAD_DIM
    tq = KB
    nq = seq // tq
    return pl.pallas_call(
        functools.partial(_attn_b_body, tq=tq),
        grid=(batch, n_heads // 2, nq),
        in_specs=[
            pl.BlockSpec((tq, w), lambda b, h, i: (b * nq + i, col_q + h)),
            pl.BlockSpec((seq, w), lambda b, h, i: (b, col_k + h)),
            pl.BlockSpec((seq, w), lambda b, h, i: (b, col_v + h)),
            pl.BlockSpec((N_META, w), lambda b, h, i: (0, col_k + h)),
            pl.BlockSpec((w, N_META), lambda b, h, i: (col_v + h, 0)),
            pl.BlockSpec((KB, KB), lambda b, h, i: (0, 0)),
            pl.BlockSpec((N_META, N_META), lambda b, h, i: (0, 0)),
        ],
        out_specs=pl.BlockSpec((tq, w), lambda b, h, i: (b * nq + i, h)),
        out_shape=jax.ShapeDtypeStruct((batch * seq, n_heads * HEAD_DIM), BF16),
        scratch_shapes=[
            pltpu.VMEM((seq // KB, w, KB), BF16),
            pltpu.VMEM((2 * tq // QG, 1, QG), F32),
            pltpu.VMEM((2 * tq // QG, HEAD_DIM, QG), F32),
        ],
        compiler_params=_cparams(("parallel", "parallel", "arbitrary")),
        name="attn_b",
    )(pb, pb, pb, pb_meta, pbt_meta, _suffix_matrix_t(KB), _suffix_matrix_t(N_META))


SAMPLE_PAGES_PER_STEP = 4
N_SUB = 8
N_LANE = 128


def _diag_mask():
    sub = lax.broadcasted_iota(jnp.int32, (N_SUB, N_LANE), 0)
    lane = lax.broadcasted_iota(jnp.int32, (N_SUB, N_LANE), 1)
    return (lane // (N_LANE // N_SUB)) == sub


def _diag_to_cols(x, diag):
    row = jnp.sum(jnp.where(diag, x, 0.0), axis=0, keepdims=True)
    return jnp.transpose(jnp.broadcast_to(row, (N_LANE, N_LANE)))


def _suffix_excl(x):
    n = x.shape[0]
    zero = jnp.zeros(x.shape[1:], F32)
    if n <= 8:
        outs, run = [None] * n, zero
        for j in range(n - 1, -1, -1):
            outs[j] = run
            run = run + x[j]
        return jnp.stack(outs, axis=0), run
    ng = n // 8
    g = x.reshape(ng, 8, N_SUB, N_LANE)
    inner, run = [None] * 8, jnp.zeros((ng, N_SUB, N_LANE), F32)
    for j in range(7, -1, -1):
        inner[j] = run
        run = run + g[:, j]
    offs, tot = [None] * ng, zero
    for i in range(ng - 1, -1, -1):
        offs[i] = tot
        tot = tot + run[i]
    off = jnp.stack(offs, axis=0)
    later = jnp.stack([w + off for w in inner], axis=1)
    return later.reshape(n, N_SUB, N_LANE), tot


def _sample_block(k_a, v_a, k_b, v_b, q_a, q_b, ntok, mask_a, mask_b, diag, state):
    m_a, l_a, acc_a, carry_b, acc_b = state
    rows = ntok * N_SUB
    tn_dims = (((0,), (0,)), ((), ()))
    s = jnp.dot(k_a, q_a, preferred_element_type=F32).reshape(ntok, N_SUB, N_LANE)
    z = jnp.dot(k_b, q_b, preferred_element_type=F32).reshape(ntok, N_SUB, N_LANE)
    if mask_a is not None:
        s = jnp.where(mask_a, s, NEG_INF)
    m_old = m_a[...]
    m_new = jnp.maximum(m_old, jnp.max(s, axis=0))
    alpha = jnp.exp2(m_old - m_new)
    p = jnp.where(diag, jnp.exp2(s - m_new), 0.0)
    l_a[...] = alpha * l_a[...] + jnp.sum(p, axis=0)
    m_a[...] = m_new
    pv_a = lax.dot_general(p.reshape(rows, N_LANE).astype(BF16), v_a, tn_dims, preferred_element_type=F32)
    lm = _neg_softplus2(z)
    ls = lm + z
    keep = diag
    if mask_b is not None:
        lm = jnp.where(mask_b, lm, 0.0)
        keep = jnp.logical_and(mask_b, diag)
    later, total = _suffix_excl(lm)
    a = jnp.where(keep, jnp.exp2(ls + later + carry_b[...]), 0.0).reshape(rows, N_LANE)
    pv_b = lax.dot_general(a.astype(BF16), v_b, tn_dims, preferred_element_type=F32)
    a_col = _diag_to_cols(alpha, diag)
    acc_a[...] = acc_a[...] * jnp.concatenate([a_col, a_col], axis=1) + pv_a
    acc_b[...] += pv_b
    carry_b[...] += total


def _sample_attn_body(pt_ref, lam_ref, qa_ref, qb_ref, nka_ref, nva_ref, nkb_ref, nvb_ref, g_ref, *rest,
                      lam_init, page, n_new, pps):
    page_refs, (oa_ref, ob_ref, m_a, l_a, acc_a, carry_b, acc_b) = rest[:4 * pps], rest[4 * pps:]
    p = pl.program_id(1)
    diag = _diag_mask()
    state = (m_a, l_a, acc_a, carry_b, acc_b)
    q_a, q_b = qa_ref[0], qb_ref[0]

    @pl.when(p == 0)
    def _():
        m_a[...] = jnp.full(m_a.shape, NEG_INF, F32)
        l_a[...] = jnp.zeros(l_a.shape, F32)
        acc_a[...] = jnp.zeros(acc_a.shape, F32)
        carry_b[...] = jnp.zeros(carry_b.shape, F32)
        acc_b[...] = jnp.zeros(acc_b.shape, F32)
        tok = lax.broadcasted_iota(jnp.int32, (n_new, N_SUB, N_LANE), 0)
        qpos = lax.broadcasted_iota(jnp.int32, (n_new, N_SUB, N_LANE), 2) % 8
        _sample_block(nka_ref[0], nva_ref[0], nkb_ref[0], nvb_ref[0], q_a, q_b, n_new,
                      tok <= qpos, tok < qpos, diag, state)

    @pl.when(p > 0)
    def _():
        rows = page * N_SUB

        def halves(ref):
            h0 = ref[0, :, pl.ds(0, N_SUB, stride=2), :].reshape(rows, HEAD_DIM).astype(BF16)
            h1 = ref[0, :, pl.ds(1, N_SUB, stride=2), :].reshape(rows, HEAD_DIM).astype(BF16)
            return h0, h1

        for j in range(pps):
            cak_ref, cav_ref, cbk_ref, cbv_ref = page_refs[4 * j:4 * j + 4]
            k_a = jnp.concatenate(halves(cak_ref), axis=1)
            v_a = cav_ref[0].reshape(rows, 2 * HEAD_DIM).astype(BF16)
            k_b = jnp.concatenate(halves(cbk_ref), axis=1)
            v_b = jnp.concatenate(halves(cbv_ref), axis=1)
            _sample_block(k_a, v_a, k_b, v_b, q_a, q_b, page, None, None, diag, state)

    @pl.when(p == pl.num_programs(1) - 1)
    def _():
        l_col = _diag_to_cols(l_a[...], diag)
        o = acc_a[...] / jnp.concatenate([l_col, l_col], axis=1)
        lam = lam_ref[0, 0]
        g = g_ref[...]
        w = 2 * HEAD_DIM
        for h in range(N_SUB):
            d = o[h * 16:h * 16 + 8] - lam * o[h * 16 + 8:h * 16 + 16]
            oa_ref[0, :, h * w:(h + 1) * w] = _subln(d, g, lam_init).astype(oa_ref.dtype)
        ob = acc_b[...]
        for h in range(2 * N_SUB):
            par = (h % 2) * HEAD_DIM
            ob_ref[0, :, h * HEAD_DIM:(h + 1) * HEAD_DIM] = ob[h * 8:(h + 1) * 8, par:par + HEAD_DIM].astype(ob_ref.dtype)


def _sample_attn(page_table, lam, q_a, q_b, nk_a, nv_a, nk_b, nv_b, g, ck_a, cv_a, ck_b, cv_b, *, lam_init):
    n_seq, n_pages = page_table.shape
    page = ck_a.shape[1]
    n_new = nk_a.shape[1] // N_SUB
    assert q_a.shape[1:] == (2 * HEAD_DIM, N_LANE) and cv_a.shape[2:] == (N_SUB, 2 * HEAD_DIM)
    assert ck_a.shape[2:] == (2 * N_SUB, HEAD_DIM) and ck_b.shape[2:] == (2 * N_SUB, HEAD_DIM)

    def seq_map(s, p, pt):
        return (s, 0, 0)

    pps = SAMPLE_PAGES_PER_STEP
    assert n_pages % pps == 0

    def page_map(j):
        return lambda s, p, pt: (pt[s, n_pages - 1 - (jnp.maximum(p - 1, 0) * pps + j)], 0, 0, 0)

    w = 2 * HEAD_DIM
    page_specs = []
    for j in range(pps):
        page_specs += [
            pl.BlockSpec((1, page, 2 * N_SUB, HEAD_DIM), page_map(j)),
            pl.BlockSpec((1, page, N_SUB, w), page_map(j)),
            pl.BlockSpec((1, page, 2 * N_SUB, HEAD_DIM), page_map(j)),
            pl.BlockSpec((1, page, 2 * N_SUB, HEAD_DIM), page_map(j)),
        ]
    body = functools.partial(_sample_attn_body, lam_init=lam_init, page=page, n_new=n_new, pps=pps)
    grid_spec = pltpu.PrefetchScalarGridSpec(
        num_scalar_prefetch=1,
        grid=(n_seq, n_pages // pps + 1),
        in_specs=[
            pl.BlockSpec(memory_space=pltpu.SMEM),
            pl.BlockSpec((1, w, N_LANE), seq_map),
            pl.BlockSpec((1, w, N_LANE), seq_map),
            pl.BlockSpec((1, n_new * N_SUB, w), seq_map),
            pl.BlockSpec((1, n_new * N_SUB, w), seq_map),
            pl.BlockSpec((1, n_new * N_SUB, w), seq_map),
            pl.BlockSpec((1, n_new * N_SUB, w), seq_map),
            pl.BlockSpec((1, w), lambda s, p, pt: (0, 0)),
        ] + page_specs,
        out_specs=[
            pl.BlockSpec((1, n_new, N_SUB * w), seq_map),
            pl.BlockSpec((1, n_new, 2 * N_SUB * HEAD_DIM), seq_map),
        ],
        scratch_shapes=[
            pltpu.VMEM((N_SUB, N_LANE), F32),
            pltpu.VMEM((N_SUB, N_LANE), F32),
            pltpu.VMEM((N_LANE, w), F32),
            pltpu.VMEM((N_SUB, N_LANE), F32),
            pltpu.VMEM((N_LANE, w), F32),
        ],
    )
    return pl.pallas_call(
        body,
        grid_spec=grid_spec,
        out_shape=[
            jax.ShapeDtypeStruct((n_seq, n_new, N_SUB * w), BF16),
            jax.ShapeDtypeStruct((n_seq, n_new, 2 * N_SUB * HEAD_DIM), BF16),
        ],
        compiler_params=_cparams(("parallel", "arbitrary")),
        name="sample_attn",
    )(page_table, lam, q_a, q_b, nk_a, nv_a, nk_b, nv_b, g, *([ck_a, cv_a, ck_b, cv_b] * pps))


def _sample_operands(pbs, n_seq, n_new, width_a, width_b):
    qa, ka, va, qb, kb, vb = jnp.split(
        pbs, [width_a, 2 * width_a, 3 * width_a, 3 * width_a + width_b, 3 * width_a + 2 * width_b], axis=1)
    eye2 = jnp.eye(2, dtype=pbs.dtype)
    qa5 = qa.reshape(n_seq, n_new, N_SUB, 2, HEAD_DIM).transpose(0, 3, 4, 2, 1)
    q_a = (qa5[:, :, :, :, None, :] * eye2[None, :, None, None, :, None]).reshape(n_seq, 2 * HEAD_DIM, N_LANE)
    qb4 = qb.reshape(n_seq, n_new, 2 * N_SUB, HEAD_DIM).transpose(0, 3, 2, 1)
    par = (jnp.arange(2 * N_SUB)[None, :] % 2 == jnp.arange(2)[:, None]).astype(pbs.dtype)
    q_b = (qb4[:, None] * par[None, :, None, :, None]).reshape(n_seq, 2 * HEAD_DIM, N_LANE)
    nk_a = ka.reshape(n_seq, n_new * N_SUB, 2 * HEAD_DIM)
    nv_a = va.reshape(n_seq, n_new * N_SUB, 2 * HEAD_DIM)
    nk_b = kb.reshape(n_seq, n_new * N_SUB, 2 * HEAD_DIM)
    nv_b = vb.reshape(n_seq, n_new * N_SUB, 2 * HEAD_DIM)
    return q_a, q_b, nk_a, nv_a, nk_b, nv_b


def _outproj_body(oap_ref, obp_ref, xp_ref, oas_ref, obs_ref, xs_ref, wa_ref, wb_ref, o_ref, *, n_p):
    i = pl.program_id(0)

    def run(oa_ref, ob_ref, x_ref):
        acc = jnp.dot(oa_ref[...], wa_ref[...], preferred_element_type=F32)
        acc = acc + jnp.dot(ob_ref[...], wb_ref[...], preferred_element_type=F32)
        o_ref[...] = x_ref[...] + acc

    @pl.when(i < n_p)
    def _():
        run(oap_ref, obp_ref, xp_ref)

    @pl.when(i >= n_p)
    def _():
        run(oas_ref, obs_ref, xs_ref)


def _outproj(oa_p, ob_p, x_p, oa_s, ob_s, x_s, w_bf, *, tm, tn):
    rp, wa = oa_p.shape
    wb = ob_p.shape[1]
    d = w_bf.shape[1]
    assert rp % tm == 0 and oa_s.shape[0] == tm and wa == wb and d % tn == 0
    n_p = rp // tm

    def pmap(i, j):
        return (jnp.minimum(i, n_p - 1), 0)

    def pmap_x(i, j):
        return (jnp.minimum(i, n_p - 1), j)

    return pl.pallas_call(
        functools.partial(_outproj_body, n_p=n_p),
        grid=(n_p + 1, d // tn),
        in_specs=[
            pl.BlockSpec((tm, wa), pmap),
            pl.BlockSpec((tm, wb), pmap),
            pl.BlockSpec((tm, tn), pmap_x),
            pl.BlockSpec((tm, wa), lambda i, j: (0, 0)),
            pl.BlockSpec((tm, wb), lambda i, j: (0, 0)),
            pl.BlockSpec((tm, tn), lambda i, j: (0, j)),
            pl.BlockSpec((wa, tn), lambda i, j: (0, j)),
            pl.BlockSpec((wb, tn), lambda i, j: (1, j)),
        ],
        out_specs=pl.BlockSpec((tm, tn), lambda i, j: (i, j)),
        out_shape=jax.ShapeDtypeStruct((rp + tm, d), F32),
        compiler_params=_cparams(("parallel", "arbitrary")),
        name="outproj",
    )(oa_p, ob_p, x_p, oa_s, ob_s, x_s, w_bf, w_bf)


def _rmsnorm_f32(x, g):
    ms = jnp.mean(x * x, axis=-1, keepdims=True)
    return x * lax.rsqrt(ms + NORM_EPS) * g


def _first_lane(cond, lane_f):
    return jnp.min(jnp.where(cond, lane_f, float(N_LANE)), axis=1, keepdims=True)


def _router_body(h_ref, g_ref, whi_ref, wlo_ref, ids_ref, wts_ref):
    n = _rmsnorm_f32(h_ref[...], g_ref[...])
    hi, lo = _split_bf16(n)
    whi = whi_ref[...]
    logits = (jnp.dot(hi, whi, preferred_element_type=F32) + jnp.dot(lo, whi, preferred_element_type=F32)
              + jnp.dot(hi, wlo_ref[...], preferred_element_type=F32))
    lane = lax.broadcasted_iota(jnp.int32, logits.shape, 1)
    lane_f = lane.astype(F32)
    is_g = lane < N_GROUPS
    gl = jnp.where(is_g, logits, NEG_INF)
    gmax = jnp.max(gl, axis=1, keepdims=True)
    gsum = jnp.sum(jnp.exp(gl - gmax), axis=1, keepdims=True)
    g_gate = 1.0 / gsum
    g_idx = _first_lane(jnp.logical_and(is_g, gl == gmax), lane_f)
    e_lo = N_GROUPS + EXPERTS_PER_GROUP * g_idx
    sel = jnp.logical_and(lane_f >= e_lo, lane_f < e_lo + EXPERTS_PER_GROUP)
    el = jnp.where(sel, logits, NEG_INF)
    emax = jnp.max(el, axis=1, keepdims=True)
    ee = jnp.exp(el - emax)
    pe = ee / jnp.sum(ee, axis=1, keepdims=True)
    p1 = jnp.max(jnp.where(sel, pe, -1.0), axis=1, keepdims=True)
    i1 = _first_lane(jnp.logical_and(sel, pe == p1), lane_f)
    rest = jnp.logical_and(sel, lane_f != i1)
    p2 = jnp.max(jnp.where(rest, pe, -1.0), axis=1, keepdims=True)
    i2 = _first_lane(jnp.logical_and(rest, pe == p2), lane_f)
    denom = p1 + p2
    w1 = g_gate * p1 / denom
    w2 = g_gate * p2 / denom
    ids_ref[...] = jnp.where(lane == 0, i1 - N_GROUPS, jnp.where(lane == 1, i2 - N_GROUPS, 0.0)).astype(jnp.int32)
    wts_ref[...] = jnp.where(lane == 0, w1, jnp.where(lane == 1, w2, 0.0))


def _router(h, n_rows, g, w_hi, w_lo, *, tm):
    d = h.shape[1]
    assert n_rows % tm == 0
    return pl.pallas_call(
        _router_body,
        grid=(n_rows // tm,),
        in_specs=[
            pl.BlockSpec((tm, d), lambda i: (i, 0)),
            pl.BlockSpec((1, d), lambda i: (0, 0)),
            pl.BlockSpec((d, N_LANE), lambda i: (0, 0)),
            pl.BlockSpec((d, N_LANE), lambda i: (0, 0)),
        ],
        out_specs=[
            pl.BlockSpec((tm, N_LANE), lambda i: (i, 0)),
            pl.BlockSpec((tm, N_LANE), lambda i: (i, 0)),
        ],
        out_shape=[
            jax.ShapeDtypeStruct((n_rows, N_LANE), jnp.int32),
            jax.ShapeDtypeStruct((n_rows, N_LANE), F32),
        ],
        compiler_params=_cparams(("parallel",)),
        name="router",
    )(h, g, w_hi, w_lo)


def _dispatch_plan(ids, *, tm, n_split):
    t = ids.shape[0]
    e_flat = ids.reshape(-1)
    n_pairs = e_flat.shape[0]
    nt = (n_pairs + N_EXPERTS * (tm - 1)) // tm
    onehot = (e_flat[:, None] == jnp.arange(N_EXPERTS)[None, :]).astype(jnp.int32)
    counts = jnp.sum(onehot, axis=0)
    rank = jnp.sum((jnp.cumsum(onehot, axis=0) - 1) * onehot, axis=1)
    ntile = (counts + tm - 1) // tm
    tile_end = jnp.cumsum(ntile)
    tile_start = tile_end - ntile
    n_active = tile_end[-1]
    dest = jnp.sum(onehot * tile_start[None, :], axis=1) * tm + rank
    order = jnp.argsort(e_flat, stable=True)
    sorted_tokens = (order // 2).astype(jnp.int32)
    pair_start = jnp.cumsum(counts) - counts

    def count_le(bounds, x):
        return jnp.sum((bounds[None, :] <= x[:, None]).astype(jnp.int32), axis=1)

    tiles = jnp.arange(nt)
    tile_expert = jnp.minimum(count_le(tile_end, tiles), N_EXPERTS - 1)
    tile_rank0 = (tiles - tile_start[tile_expert]) * tm
    tile_src = pair_start[tile_expert] + tile_rank0
    tile_valid = jnp.where(tiles < n_active, jnp.clip(counts[tile_expert] - tile_rank0, 0, tm), 0)
    tile_tab = jnp.stack([tile_src, tile_valid], axis=0).astype(jnp.int32)
    n_items = nt * n_split
    n_act_items = n_active * n_split
    item_end = tile_end * n_split
    idx = jnp.arange(n_items)
    live = idx < n_act_items
    w = jnp.minimum(idx, n_act_items - 1)
    w_exp = jnp.minimum(count_le(item_end, w), N_EXPERTS - 1)
    local = w - tile_start[w_exp] * n_split
    w_nt = jnp.maximum(ntile[w_exp], 1)
    w_split = local // w_nt
    idle = idx - n_act_items
    o_tile = jnp.where(live, tile_start[w_exp] + local % w_nt, n_active + idle // n_split)
    o_split = jnp.where(live, w_split, idle % n_split)
    w_first = jnp.logical_and(local % w_nt == 0, live)
    items = jnp.stack([w_exp, w_split, o_tile, w_first.astype(jnp.int32), o_split], axis=0).astype(jnp.int32)
    counts_i = jnp.stack([n_active, n_active * n_split]).astype(jnp.int32)
    return dest.astype(jnp.int32), sorted_tokens, tile_tab, items, counts_i


ISSUE_UNROLL = 4
ROW_CHUNK = 64


def _gather_norm_body(cnt_ref, tab_ref, stok_ref, g_ref, h_hbm, o_ref, buf, sem, *, tm):
    i = pl.program_id(0)
    n_act = cnt_ref[0]

    def row_copy(t, r, slot):
        src = stok_ref[tab_ref[0, t] + r]
        return pltpu.make_async_copy(h_hbm.at[pl.ds(src, 1)], buf.at[slot, pl.ds(r, 1)], sem.at[slot])

    def issue(t, slot):
        n_real = tab_ref[1, t]

        def start_group(q, c):
            for u in range(ISSUE_UNROLL):
                row_copy(t, q * ISSUE_UNROLL + u, slot).start()
            return c

        def start(r, c):
            row_copy(t, r, slot).start()
            return c

        n_groups = n_real // ISSUE_UNROLL
        lax.fori_loop(0, n_groups, start_group, 0)
        lax.fori_loop(n_groups * ISSUE_UNROLL, n_real, start, 0)

    @pl.when(i == 0)
    def _():
        buf[...] = jnp.zeros(buf.shape, F32)
        issue(0, 0)

    @pl.when(i + 1 < n_act)
    def _():
        issue(i + 1, (i + 1) % 2)

    @pl.when(i < n_act)
    def _():
        slot = i % 2
        n_real = tab_ref[1, i]

        @pl.when(n_real == tm)
        def _():
            pltpu.make_async_copy(h_hbm.at[pl.ds(0, tm)], buf.at[slot], sem.at[slot]).wait()

        @pl.when(n_real < tm)
        def _():
            def wait(r, c):
                row_copy(i, r, slot).wait()
                return c

            lax.fori_loop(0, n_real, wait, 0)

        g = g_ref[...]

        def chunk(c, carry):
            r0 = pl.multiple_of(c * ROW_CHUNK, ROW_CHUNK)
            o_ref[pl.ds(r0, ROW_CHUNK), :] = _rmsnorm_f32(buf[slot, pl.ds(r0, ROW_CHUNK), :], g).astype(o_ref.dtype)
            return carry

        lax.fori_loop(0, tm // ROW_CHUNK, chunk, 0)

    @pl.when(i >= n_act)
    def _():
        o_ref[...] = jnp.zeros(o_ref.shape, o_ref.dtype)


def _gather_norm(counts_i, tile_tab, sorted_tokens, g, h, *, tm):
    nt = tile_tab.shape[1]
    d = h.shape[1]
    grid_spec = pltpu.PrefetchScalarGridSpec(
        num_scalar_prefetch=3,
        grid=(nt,),
        in_specs=[
            pl.BlockSpec((1, d), lambda i, *_: (0, 0)),
            pl.BlockSpec(memory_space=pl.ANY),
        ],
        out_specs=pl.BlockSpec((tm, d), lambda i, *_: (i, 0)),
        scratch_shapes=[pltpu.VMEM((2, tm, d), F32), pltpu.SemaphoreType.DMA((2,))],
    )
    return pl.pallas_call(
        functools.partial(_gather_norm_body, tm=tm),
        grid_spec=grid_spec,
        out_shape=jax.ShapeDtypeStruct((nt * tm, d), BF16),
        compiler_params=_cparams(("arbitrary",)),
        name="gather_norm",
    )(counts_i, tile_tab, sorted_tokens, g, h)


def _expert_up_body(items_ref, cnt_ref, x_ref, wg_ref, wu_ref, o_ref, wg_s, wu_s):
    w = pl.program_id(0)

    @pl.when(items_ref[3, w] == 1)
    def _():
        wg_s[...] = wg_ref[0].astype(BF16)
        wu_s[...] = wu_ref[0].astype(BF16)

    @pl.when(w < cnt_ref[1])
    def _():
        x = x_ref[...]
        a = jnp.dot(x, wg_s[...], preferred_element_type=F32)
        u = jnp.dot(x, wu_s[...], preferred_element_type=F32)
        o_ref[...] = (a * (1.0 / (1.0 + jnp.exp(-a))) * u).astype(o_ref.dtype)

    @pl.when(w >= cnt_ref[1])
    def _():
        o_ref[...] = jnp.zeros(o_ref.shape, o_ref.dtype)


def _expert_up(items, counts_i, xs, w_gate, w_up, *, tm, tf):
    n_items = items.shape[1]
    d, ff = w_gate.shape[1:]
    grid_spec = pltpu.PrefetchScalarGridSpec(
        num_scalar_prefetch=2,
        grid=(n_items,),
        in_specs=[
            pl.BlockSpec((tm, d), lambda w, it, c: (it[2, w], 0)),
            pl.BlockSpec((1, d, tf), lambda w, it, c: (it[0, w], 0, it[1, w])),
            pl.BlockSpec((1, d, tf), lambda w, it, c: (it[0, w], 0, it[1, w])),
        ],
        out_specs=pl.BlockSpec((tm, tf), lambda w, it, c: (it[2, w], it[4, w])),
        scratch_shapes=[pltpu.VMEM((d, tf), BF16), pltpu.VMEM((d, tf), BF16)],
    )
    return pl.pallas_call(
        _expert_up_body,
        grid_spec=grid_spec,
        out_shape=jax.ShapeDtypeStruct((xs.shape[0], ff), BF16),
        compiler_params=_cparams(("arbitrary",)),
        name="expert_up",
    )(items, counts_i, xs, w_gate, w_up)


def _expert_down_body(items_ref, cnt_ref, a_ref, wd_ref, o_ref, wd_s):
    w = pl.program_id(0)

    @pl.when(items_ref[3, w] == 1)
    def _():
        wd_s[...] = wd_ref[0].astype(BF16)

    @pl.when(w < cnt_ref[1])
    def _():
        o_ref[...] = jnp.dot(a_ref[...], wd_s[...], preferred_element_type=F32)

    @pl.when(w >= cnt_ref[1])
    def _():
        o_ref[...] = jnp.zeros(o_ref.shape, o_ref.dtype)


def _expert_down(items, counts_i, act, w_down, *, tm, tn):
    n_items = items.shape[1]
    ff, d = w_down.shape[1:]
    grid_spec = pltpu.PrefetchScalarGridSpec(
        num_scalar_prefetch=2,
        grid=(n_items,),
        in_specs=[
            pl.BlockSpec((tm, ff), lambda w, it, c: (it[2, w], 0)),
            pl.BlockSpec((1, ff, tn), lambda w, it, c: (it[0, w], 0, it[1, w])),
        ],
        out_specs=pl.BlockSpec((tm, tn), lambda w, it, c: (it[2, w], it[4, w])),
        scratch_shapes=[pltpu.VMEM((ff, tn), BF16)],
    )
    return pl.pallas_call(
        _expert_down_body,
        grid_spec=grid_spec,
        out_shape=jax.ShapeDtypeStruct((act.shape[0], d), F32),
        compiler_params=_cparams(("arbitrary",)),
        name="expert_down",
    )(items, counts_i, act, w_down)


def _combine_body(dest_ref, h_ref, wts_ref, g_ref, y_hbm, op_ref, os_ref, buf, sem, *, tm, n_p):
    i = pl.program_id(0)

    def row_copy(t, r, k, slot):
        src = dest_ref[(t * tm + r) * 2 + k]
        return pltpu.make_async_copy(y_hbm.at[pl.ds(src, 1)], buf.at[slot, k, pl.ds(r, 1)], sem.at[slot])

    def issue(t, slot):
        def start_group(q, c):
            for u in range(ISSUE_UNROLL):
                row_copy(t, q * ISSUE_UNROLL + u, 0, slot).start()
                row_copy(t, q * ISSUE_UNROLL + u, 1, slot).start()
            return c

        lax.fori_loop(0, tm // ISSUE_UNROLL, start_group, 0)

    @pl.when(i == 0)
    def _():
        issue(0, 0)

    @pl.when(i + 1 < pl.num_programs(0))
    def _():
        issue(i + 1, (i + 1) % 2)

    slot = i % 2
    for k in range(2):
        pltpu.make_async_copy(y_hbm.at[pl.ds(0, tm)], buf.at[slot, k], sem.at[slot]).wait()
    g = g_ref[...]

    def rows(o_ref):
        def chunk(c, carry):
            r0 = pl.multiple_of(c * ROW_CHUNK, ROW_CHUNK)
            sl = pl.ds(r0, ROW_CHUNK)
            wts = wts_ref[sl, :]
            out = h_ref[sl, :] + (wts[:, 0:1] * buf[slot, 0, sl, :] + wts[:, 1:2] * buf[slot, 1, sl, :])
            o_ref[sl, :] = _rmsnorm_f32(out, g)
            return carry

        lax.fori_loop(0, tm // ROW_CHUNK, chunk, 0)

    @pl.when(i < n_p)
    def _():
        rows(op_ref)

    @pl.when(i >= n_p)
    def _():
        rows(os_ref)


def _combine(dest, h, wts, g, y, *, tm, n_p, n_s):
    d = h.shape[1]
    grid_spec = pltpu.PrefetchScalarGridSpec(
        num_scalar_prefetch=1,
        grid=(n_p + n_s,),
        in_specs=[
            pl.BlockSpec((tm, d), lambda i, *_: (i, 0)),
            pl.BlockSpec((tm, N_LANE), lambda i, *_: (i, 0)),
            pl.BlockSpec((1, d), lambda i, *_: (0, 0)),
            pl.BlockSpec(memory_space=pl.ANY),
        ],
        out_specs=[
            pl.BlockSpec((tm, d), lambda i, *_: (jnp.minimum(i, n_p - 1), 0)),
            pl.BlockSpec((tm, d), lambda i, *_: (jnp.maximum(i - n_p, 0), 0)),
        ],
        scratch_shapes=[pltpu.VMEM((2, 2, tm, d), F32), pltpu.SemaphoreType.DMA((2,))],
    )
    return pl.pallas_call(
        functools.partial(_combine_body, tm=tm, n_p=n_p),
        grid_spec=grid_spec,
        out_shape=[
            jax.ShapeDtypeStruct((n_p * tm, d), F32),
            jax.ShapeDtypeStruct((n_s * tm, d), F32),
        ],
        compiler_params=_cparams(("arbitrary",)),
        name="combine",
    )(dest, h, wts, g, y)


def kernel(x_prompt, x_sample, cache_a_k, cache_a_v, cache_b_k, cache_b_v, page_table, meta_tokens, attn_norm, w_in, lam_q1, lam_k1, lam_q2, lam_k2, subln_norm, w_out, ffn_norm, w_router_group, w_router_expert, w_gate, w_up, w_down, final_norm):
    assert w_in.shape[0] == 1, "single-layer step"
    layer = 0
    b, seq, d = x_prompt.shape
    n_seq, n_new, _ = x_sample.shape
    n_pool, page = cache_a_k.shape[1], cache_a_k.shape[2]
    n_heads_a, n_heads_b = cache_a_v.shape[3], cache_b_k.shape[3]
    width_a, width_b = n_heads_a * 2 * HEAD_DIM, n_heads_b * HEAD_DIM
    past_len = page_table.shape[1] * page
    rows_p, rows_s = b * seq, n_seq * n_new
    wblk = 2 * HEAD_DIM

    lam_init = 0.8 - 0.6 * math.exp(-0.3 * layer)
    lam = (jnp.exp(jnp.sum(lam_q1[layer].astype(F32) * lam_k1[layer].astype(F32)))
           - jnp.exp(jnp.sum(lam_q2[layer].astype(F32) * lam_k2[layer].astype(F32))) + lam_init).reshape(1, 1)
    g_attn = attn_norm[layer].reshape(1, d)
    g_subln = subln_norm[layer].reshape(1, wblk)
    g_ffn = ffn_norm[layer].reshape(1, d)

    x_small = jnp.concatenate([x_sample.reshape(rows_s, d), meta_tokens.astype(F32)], axis=0)
    pos_small = jnp.concatenate([past_len + jnp.tile(jnp.arange(n_new), n_seq), jnp.arange(N_META)])
    cos_s, sin_s = _rope_tables(pos_small)
    pbs, kvs, w_in_bf = _inproj_small(x_small, g_attn, w_in[layer], cos_s, sin_s, tn=512,
                                      width_a=width_a, width_b=width_b)
    pb_meta = pbs[rows_s:]
    xp = x_prompt.reshape(rows_p, d)
    cos_p, sin_p = _rope_tables(jnp.tile(N_META + jnp.arange(seq), b))
    meta_kv = kvs[rows_s:].reshape(N_META, 4, N_CHUNK, HEAD_DIM).transpose(1, 0, 2, 3)
    pb, *prompt_caches = _inproj_prompt(xp, g_attn, w_in_bf, cos_p, sin_p, meta_kv, tm=512, tn=1024, width_a=width_a,
                                        width_b=width_b, batch=b, seq=seq, n_meta=N_META)

    ca, cb = width_a // wblk, width_b // wblk
    pbt_meta = pb_meta.T
    oa_p = _attn_a(lam, pb, pb_meta, pbt_meta, g_subln, batch=b, seq=seq, n_heads=n_heads_a,
                    col_q=0, col_k=ca, col_v=2 * ca, lam_init=lam_init)
    ob_p = _attn_b(pb, pb_meta, pbt_meta, batch=b, seq=seq, n_heads=n_heads_b,
                    col_q=3 * ca, col_k=3 * ca + cb, col_v=3 * ca + 2 * cb)
    ops = _sample_operands(pbs[:rows_s], n_seq, n_new, width_a, width_b)
    oa_s, ob_s = _sample_attn(
        page_table, lam, *ops, g_subln,
        cache_a_k[layer].reshape(n_pool, page, 2 * n_heads_a, HEAD_DIM),
        cache_a_v[layer].reshape(n_pool, page, n_heads_a, wblk),
        cache_b_k[layer].reshape(n_pool, page, n_heads_b, HEAD_DIM),
        cache_b_v[layer].reshape(n_pool, page, n_heads_b, HEAD_DIM), lam_init=lam_init)

    tm_o = 512
    pad_s = lambda a: jnp.pad(a, ((0, tm_o - rows_s), (0, 0)))
    h = _outproj(oa_p, ob_p, xp, pad_s(oa_s.reshape(rows_s, width_a)), pad_s(ob_s.reshape(rows_s, width_b)),
                 pad_s(x_sample.reshape(rows_s, d)), w_out[layer].astype(BF16), tm=tm_o, tn=1024)

    tm_e = 256
    n_tok = rows_p + rows_s
    w_r = jnp.concatenate([w_router_group[layer], w_router_expert[layer].reshape(d, N_EXPERTS)], axis=1).astype(F32)
    w_r = jnp.pad(w_r, ((0, 0), (0, N_LANE - w_r.shape[1])))
    w_r_hi, w_r_lo = _split_bf16(w_r)
    ids, wts = _router(h, n_tok, g_ffn, w_r_hi, w_r_lo, tm=tm_e)
    n_split = 2
    dest, sorted_tokens, tile_tab, items, counts_i = _dispatch_plan(ids[:, :2], tm=tm_e, n_split=n_split)
    xs_sorted = _gather_norm(counts_i, tile_tab, sorted_tokens, g_ffn, h, tm=tm_e)
    act = _expert_up(items, counts_i, xs_sorted, w_gate[layer], w_up[layer], tm=tm_e, tf=w_gate.shape[3] // n_split)
    y = _expert_down(items, counts_i, act, w_down[layer], tm=tm_e, tn=d // n_split)
    y_p, y_s = _combine(dest, h, wts, final_norm.reshape(1, d), y,
                        tm=tm_e, n_p=rows_p // tm_e, n_s=rows_s // tm_e)

    def prompt_cache(cache, lo, width, shape):
        return cache.reshape((1, b, seq + N_META) + shape)

    def sample_cache(lo, width, shape):
        return kvs[:rows_s, lo:lo + width].reshape((1, n_seq, n_new) + shape)

    shapes = [(0, width_a, (n_heads_a, 2, HEAD_DIM)), (width_a, width_a, (n_heads_a, wblk)),
              (2 * width_a, width_b, (n_heads_b, HEAD_DIM)), (2 * width_a + width_b, width_b, (n_heads_b, HEAD_DIM))]
    return ((y_p.reshape(b, seq, d), y_s.reshape(n_seq, n_new, d))
            + tuple(prompt_cache(c, *s) for c, s in zip(prompt_caches, shapes))
            + tuple(sample_cache(*s) for s in shapes))
```
